```python
import jax, jax.numpy as jnp
from jax import lax
import numpy as np

D_MODEL = 2048
BATCH = 8
SEQ = 2048
DEPTH = 1

EPS = 1e-6

POOL_WINDOWS = (2, 4, 8, 16)
POOL_GROUPS = len(POOL_WINDOWS)
POOL_WIDTH = D_MODEL // 2
POOL_GROUP_DIM = POOL_WIDTH // POOL_GROUPS
POOL_OUT_DIM = D_MODEL // POOL_GROUPS

HG_EXPAND = 128
HG_HEADS = D_MODEL // HG_EXPAND
HG_HEAD_V = D_MODEL // HG_HEADS
HG_KEY_WIDTH = HG_HEADS * HG_EXPAND
HG_VAL_WIDTH = HG_HEADS * HG_HEAD_V
HG_CHUNK = 64

IN_SPLIT_WIDTHS = (POOL_WIDTH, HG_KEY_WIDTH, HG_KEY_WIDTH, HG_VAL_WIDTH, HG_VAL_WIDTH, D_MODEL, D_MODEL)
IN_WIDTH = sum(IN_SPLIT_WIDTHS)

PEER_HEADS = 8
PEER_NKEYS = 128
PEER_EXPERTS = PEER_NKEYS * PEER_NKEYS
PEER_DKEY = 256
PEER_TOPK = 16
PEER_BLOCK = 128

kernel_name = "hybrid_pool_hgrn2_peer_adaln"


def rms_norm(x, gain):
    xf = x.astype(jnp.float32)
    y = xf * lax.rsqrt(jnp.mean(xf * xf, axis=-1, keepdims=True) + EPS)
    return (y * gain.astype(jnp.float32)).astype(x.dtype)


def modulate(h, shift, scale):
    return h * (1.0 + scale[:, None, :]) + shift[:, None, :]


def pool_mixer(p, pool_w, pool_scale):
    b, s, _ = p.shape
    pf = p.astype(jnp.float32)
    cs = jnp.cumsum(pf, axis=1)
    pos = jnp.arange(s)
    outs = []
    for g, w in enumerate(POOL_WINDOWS):
        lo, hi = g * POOL_GROUP_DIM, (g + 1) * POOL_GROUP_DIM
        csg = cs[..., lo:hi]
        prev = jnp.pad(csg, ((0, 0), (w, 0), (0, 0)))[:, :s]
        cnt = jnp.minimum(pos + 1, w).astype(jnp.float32)[None, :, None]
        outs.append((csg - prev) / cnt - pf[..., lo:hi])
    pooled = jnp.stack(outs, axis=2)
    y = jnp.einsum('bsgd,gde->bsge', pooled, pool_w.astype(jnp.float32))
    y = y.reshape(b, s, D_MODEL) * pool_scale.astype(jnp.float32)
    return y.astype(p.dtype)


def hgrn2_mixer(q, f_logit, i, g, lb, hg_norm):
    b, s, _ = q.shape
    n = s // HG_CHUNK
    lbf = lb.astype(jnp.float32)
    log_f = jnp.logaddexp(jnp.log(lbf), jnp.log1p(-lbf) + jax.nn.log_sigmoid(f_logit.astype(jnp.float32)))
    k = -jnp.expm1(log_f)

    def to_chunks(t, d):
        return t.astype(jnp.float32).reshape(b, n, HG_CHUNK, HG_HEADS, d).transpose(1, 0, 3, 2, 4)

    qc = to_chunks(q, HG_EXPAND)
    kc = to_chunks(k, HG_EXPAND)
    lfc = to_chunks(log_f, HG_EXPAND)
    ic = to_chunks(i, HG_HEAD_V)
    mask = jnp.tril(jnp.ones((HG_CHUNK, HG_CHUNK), dtype=bool))[:, :, None]

    def step(state, inp):
        qx, kx, lfx, ix = inp
        cum = jnp.cumsum(lfx, axis=2)
        diff = cum[:, :, :, None, :] - cum[:, :, None, :, :]
        decay = jnp.exp(jnp.where(mask, diff, -jnp.inf))
        attn = jnp.einsum('bhtk,bhsk,bhtsk->bhts', qx, kx, decay)
        o = jnp.einsum('bhts,bhsv->bhtv', attn, ix) + jnp.einsum('bhtk,bhkv->bhtv', qx * jnp.exp(cum), state)
        last = cum[:, :, -1:, :]
        new_state = jnp.exp(last[:, :, 0, :])[..., None] * state + jnp.einsum('bhsk,bhsv->bhkv', kx * jnp.exp(last - cum), ix)
        return new_state, o

    state0 = jnp.zeros((b, HG_HEADS, HG_EXPAND, HG_HEAD_V), jnp.float32)
    _, o = lax.scan(step, state0, (qc, kc, lfc, ic))
    o = o.transpose(1, 0, 3, 2, 4).reshape(b, s, HG_HEADS, HG_HEAD_V)
    o = o * lax.rsqrt(jnp.mean(o * o, axis=-1, keepdims=True) + EPS) * hg_norm.astype(jnp.float32)
    o = o.reshape(b, s, HG_VAL_WIDTH) * jax.nn.silu(g.astype(jnp.float32))
    return o.astype(q.dtype)


def peer_ffn(h, w_q, sub_keys, u, v):
    b, s, d = h.shape
    t = b * s
    hf = h.reshape(t, d)
    q = (hf @ w_q).astype(jnp.float32).reshape(t, PEER_HEADS, 2, PEER_DKEY // 2)
    scores = jnp.einsum('thpd,hpnd->thpn', q, sub_keys.astype(jnp.float32))
    vals, idx = lax.top_k(scores, PEER_TOPK)
    cand = (vals[:, :, 0, :, None] + vals[:, :, 1, None, :]).reshape(t, PEER_HEADS, PEER_TOPK * PEER_TOPK)
    cand_idx = (idx[:, :, 0, :, None] * PEER_NKEYS + idx[:, :, 1, None, :]).reshape(t, PEER_HEADS, PEER_TOPK * PEER_TOPK)
    top_s, sel = lax.top_k(cand, PEER_TOPK)
    experts = jnp.take_along_axis(cand_idx, sel, axis=-1)
    gates = jax.nn.softmax(top_s, axis=-1)
    nb = t // PEER_BLOCK
    xb = hf.reshape(nb, PEER_BLOCK, d)
    eb = experts.reshape(nb, PEER_BLOCK, PEER_HEADS * PEER_TOPK)
    gb = gates.reshape(nb, PEER_BLOCK, PEER_HEADS * PEER_TOPK)

    def block_fn(args):
        xk, ek, gk = args
        hid = jax.nn.gelu(jnp.einsum('td,ted->te', xk, u[ek]).astype(jnp.float32), approximate=False)
        return jnp.einsum('te,ted->td', (gk * hid).astype(v.dtype), v[ek])

    y = lax.map(block_fn, (xb, eb, gb))
    return y.reshape(b, s, d).astype(h.dtype)


def setup_inputs(seed: int = 0) -> dict:
    key = jax.random.key(seed)
    ks = jax.random.split(key, 20)
    f32 = jnp.float32
    nrm = lambda k, shape, sc: jax.random.normal(k, shape, f32) * sc
    return {
        "x": nrm(ks[0], (BATCH, SEQ, D_MODEL), 1.0),
        "c": nrm(ks[1], (BATCH, D_MODEL), 1.0),
        "w_ada": nrm(ks[2], (DEPTH, D_MODEL, 6 * D_MODEL), 0.1 * D_MODEL ** -0.5),
        "b_ada": nrm(ks[3], (DEPTH, 6 * D_MODEL), 0.01),
        "norm1": 1.0 + nrm(ks[4], (DEPTH, D_MODEL), 0.02),
        "w_in": nrm(ks[5], (DEPTH, D_MODEL, IN_WIDTH), D_MODEL ** -0.5),
        "pool_w": nrm(ks[6], (DEPTH, POOL_GROUPS, POOL_GROUP_DIM, POOL_OUT_DIM), POOL_GROUP_DIM ** -0.5),
        "pool_scale": 1.0 + nrm(ks[7], (DEPTH, D_MODEL), 0.02),
        "lb_logits": nrm(ks[8], (DEPTH + 1, HG_KEY_WIDTH), 0.5),
        "hg_norm": 1.0 + nrm(ks[9], (DEPTH, HG_HEADS, HG_HEAD_V), 0.02),
        "w_out": nrm(ks[10], (DEPTH, D_MODEL, D_MODEL), D_MODEL ** -0.5),
        "norm2": 1.0 + nrm(ks[11], (DEPTH, D_MODEL), 0.02),
        "peer_wq": nrm(ks[12], (DEPTH, D_MODEL, PEER_HEADS * PEER_DKEY), D_MODEL ** -0.5),
        "peer_keys": nrm(ks[13], (DEPTH, PEER_HEADS, 2, PEER_NKEYS, PEER_DKEY // 2), (PEER_DKEY // 2) ** -0.5),
        "peer_u": nrm(ks[14], (DEPTH, PEER_EXPERTS, D_MODEL), D_MODEL ** -0.5),
        "peer_v": nrm(ks[15], (DEPTH, PEER_EXPERTS, D_MODEL), PEER_HEADS ** -0.5),
        "final_norm": 1.0 + nrm(ks[16], (D_MODEL,), 0.02),
    }


def reference(x, c, w_ada, b_ada, norm1, w_in, pool_w, pool_scale, lb_logits, hg_norm, w_out,
              norm2, peer_wq, peer_keys, peer_u, peer_v, final_norm):
    lb_all = jnp.cumsum(jax.nn.softmax(lb_logits.astype(jnp.float32), axis=0), axis=0)
    split_at = [int(v) for v in np.cumsum(IN_SPLIT_WIDTHS)[:-1]]
    cs = jax.nn.silu(c)
    for l in range(DEPTH):
        ada = cs @ w_ada[l] + b_ada[l]
        shift1, scale1, gate1, shift2, scale2, gate2 = jnp.split(ada, 6, axis=-1)
        h = modulate(rms_norm(x, norm1[l]), shift1, scale1)
        proj = h @ w_in[l]
        p_pool, p_q, p_f, p_i, p_g, g_a, g_b = jnp.split(proj, split_at, axis=-1)
        y_a = pool_mixer(p_pool, pool_w[l], pool_scale[l])
        y_b = hgrn2_mixer(p_q, p_f, p_i, p_g, lb_all[l], hg_norm[l])
        merged = jax.nn.sigmoid(g_a) * y_a + jax.nn.sigmoid(g_b) * y_b
        x = x + (1.0 + gate1)[:, None, :] * (merged @ w_out[l])
        h2 = modulate(rms_norm(x, norm2[l]), shift2, scale2)
        x = x + (1.0 + gate2)[:, None, :] * peer_ffn(h2, peer_wq[l], peer_keys[l], peer_u[l], peer_v[l])
    return rms_norm(x, final_norm)
```

```python
import functools

import numpy as np
import jax
import jax.numpy as jnp
from jax import lax
from jax.experimental import pallas as pl
from jax.experimental.pallas import tpu as pltpu

F32 = jnp.float32
BF16 = jnp.bfloat16
EPS = 1e-6

D_MODEL = 2048
LANES = 128
POOL_WINDOWS = (2, 4, 8, 16)
POOL_WIDTH = D_MODEL // 2
POOL_GROUP_DIM = POOL_WIDTH // len(POOL_WINDOWS)
POOL_OUT_DIM = D_MODEL // len(POOL_WINDOWS)
POOL_HISTORY = 16
HG_HEADS = 16
HG_DIM = 128
HG_CHUNK = 64
HG_LEVELS = (32, 16, 8, 4, 2, 1)
PEER_HEADS = 8
PEER_NKEYS = 128
PEER_TOPK = 16
PEER_EXPERTS = PEER_NKEYS * PEER_NKEYS
INV_SQRT2 = 0.7071067811865476

OFF_POOL = 0
OFF_Q = OFF_POOL + POOL_WIDTH
OFF_F = OFF_Q + D_MODEL
OFF_I = OFF_F + D_MODEL
OFF_G = OFF_I + D_MODEL
OFF_GA = OFF_G + D_MODEL
OFF_GB = OFF_GA + D_MODEL
IN_WIDTH = OFF_GB + D_MODEL

VMEM_LIMIT_BYTES = 56 * 1024 * 1024

NT_DIMS = (((1,), (1,)), ((), ()))
TN_DIMS = (((0,), (0,)), ((), ()))


def _tiles(batch, seq):
    tokens = batch * seq
    return dict(
        ada_tn=1024,
        proj_tm=min(1024, seq), proj_tn=1024,
        mix_ts=min(256, seq),
        out_tm=min(256, seq),
        route_tl=LANES,
        peer_tb=min(512, seq), peer_eb=1024,
        tokens=tokens,
    )


def _params(semantics):
    return pltpu.CompilerParams(dimension_semantics=semantics, vmem_limit_bytes=VMEM_LIMIT_BYTES)


def _rms(x):
    return x * lax.rsqrt(jnp.mean(x * x, axis=-1, keepdims=True) + EPS)


def _ada_kernel(c_ref, w_ref, b_ref, o_ref):
    c = c_ref[...]
    cs = c * jax.nn.sigmoid(c)
    o_ref[...] = jnp.dot(cs.astype(BF16), w_ref[...].astype(BF16), preferred_element_type=F32) + b_ref[...]


def _ada(c, w, b, tn):
    bsz, n = c.shape[0], w.shape[1]
    return pl.pallas_call(
        _ada_kernel,
        out_shape=jax.ShapeDtypeStruct((bsz, n), F32),
        grid=(n // tn,),
        in_specs=[pl.BlockSpec((bsz, D_MODEL), lambda j: (0, 0)),
                  pl.BlockSpec((D_MODEL, tn), lambda j: (0, j)),
                  pl.BlockSpec((1, tn), lambda j: (0, j))],
        out_specs=pl.BlockSpec((bsz, tn), lambda j: (0, j)),
        compiler_params=_params(("arbitrary",)),
        name="ada",
    )(c, w, b)


def _proj_kernel(x_ref, ada_ref, g_ref, w_ref, o_ref, h_ref):
    @pl.when(pl.program_id(1) == 0)
    def _():
        y = _rms(x_ref[...]) * g_ref[...]
        h_ref[...] = (y * (1.0 + ada_ref[1:2, :]) + ada_ref[0:1, :]).astype(BF16)

    o_ref[...] = jnp.dot(h_ref[...], w_ref[...], preferred_element_type=F32).astype(o_ref.dtype)


def _proj(x2d, ada3, gain, w, seq, tm, tn):
    tokens, n = x2d.shape[0], w.shape[1]
    per_batch = seq // tm
    return pl.pallas_call(
        _proj_kernel,
        out_shape=jax.ShapeDtypeStruct((tokens, n), BF16),
        grid=(tokens // tm, n // tn),
        in_specs=[pl.BlockSpec((tm, D_MODEL), lambda i, j: (i, 0)),
                  pl.BlockSpec((None, 6, D_MODEL), lambda i, j: (i // per_batch, 0, 0)),
                  pl.BlockSpec((1, D_MODEL), lambda i, j: (0, 0)),
                  pl.BlockSpec((D_MODEL, tn), lambda i, j: (0, j))],
        out_specs=pl.BlockSpec((tm, tn), lambda i, j: (i, j)),
        scratch_shapes=[pltpu.VMEM((tm, D_MODEL), BF16)],
        compiler_params=_params(("arbitrary", "arbitrary")),
        name="proj",
    )(x2d, ada3, gain, w)


def _decay_sum_matrix():
    c = HG_CHUNK
    groups = []
    for m in HG_LEVELS:
        g = np.zeros((c, c), np.float32)
        for t in range(c):
            ref = (t // (2 * m)) * 2 * m + m - 1
            if (t // m) % 2 == 1:
                g[t, ref + 1:t + 1] = 1.0
            else:
                g[t, t + 1:ref + 1] = 1.0
        groups.append(g)
    groups.append(np.tril(np.ones((c, c), np.float32)))
    groups.append(np.triu(np.ones((c, c), np.float32), 1))
    n = np.concatenate(groups, axis=0)
    return np.concatenate([n, n, n], axis=1)


def _hgrn_head_chunk(q, z, iv, lb, nmat, st, row, row_c, col_c):
    c = HG_CHUNK
    t = jnp.exp(-jnp.abs(z))
    r = 1.0 / (1.0 + t)
    pos = z >= 0.0
    sig_pos = jnp.where(pos, r, t * r)
    sig_neg = jnp.where(pos, t * r, r)
    lf = jnp.log(lb + (1.0 - lb) * sig_pos)
    kk = (1.0 - lb) * sig_neg
    hi = lf.astype(BF16)
    rem = lf - hi.astype(F32)
    mid = rem.astype(BF16)
    lo = (rem - mid.astype(F32)).astype(BF16)
    dec = jnp.exp(jnp.dot(nmat, jnp.concatenate([hi, mid, lo], axis=0), preferred_element_type=F32))

    attn = jnp.zeros((c, c), F32)
    for li, m in enumerate(HG_LEVELS):
        dl = dec[li * c:(li + 1) * c]
        right = (row & m) != 0
        xs = jnp.where(right, q, kk) * dl
        ql = jnp.where(right, xs, 0.0).astype(BF16)
        kl = jnp.where(right, 0.0, xs).astype(BF16)
        a = lax.dot_general(ql, kl, NT_DIMS, preferred_element_type=F32)
        if 2 * m < c:
            sh = int(np.log2(2 * m))
            a = jnp.where((row_c >> sh) == (col_c >> sh), a, 0.0)
        attn = attn + a
    dc = dec[6 * c:7 * c]
    dr = dec[7 * c:8 * c]
    diag = jnp.sum(q * kk, axis=-1, keepdims=True)
    ivb = iv.astype(BF16)
    o = (jnp.dot(attn.astype(BF16), ivb, preferred_element_type=F32) + diag * iv
         + lax.dot_general((q * dc).astype(BF16), st.astype(BF16), NT_DIMS, preferred_element_type=F32))
    upd = lax.dot_general(ivb, (kk * dr).astype(BF16), TN_DIMS, preferred_element_type=F32)
    return o, st * dc[c - 1:c, :] + upd


def _mixer_kernel(proj_ref, lbl_ref, hgn_ref, pw_ref, ps_ref, n_ref, o_ref, st_ref, pbuf_ref, ya_ref, *, ts):
    sblk = pl.program_id(1)
    hist = POOL_HISTORY

    @pl.when(sblk == 0)
    def _():
        st_ref[...] = jnp.zeros_like(st_ref)
        pbuf_ref[0:hist, :] = jnp.zeros((hist, POOL_WIDTH), F32)

    pbuf_ref[hist:hist + ts, :] = proj_ref[:, OFF_POOL:OFF_POOL + POOL_WIDTH].astype(F32)
    pos1 = (sblk * ts + lax.broadcasted_iota(jnp.int32, (ts, 1), 0) + 1).astype(F32)
    for g, w in enumerate(POOL_WINDOWS):
        cols = slice(g * POOL_GROUP_DIM, (g + 1) * POOL_GROUP_DIM)
        cur = pbuf_ref[hist:hist + ts, cols]
        acc = cur
        for back in range(1, w):
            acc = acc + pbuf_ref[hist - back:hist - back + ts, cols]
        pooled = acc / jnp.minimum(pos1, float(w)) - cur
        oc = slice(g * POOL_OUT_DIM, (g + 1) * POOL_OUT_DIM)
        ya_ref[:, oc] = jnp.dot(pooled.astype(BF16), pw_ref[g], preferred_element_type=F32) * ps_ref[:, oc]
    pbuf_ref[0:hist, :] = pbuf_ref[ts:ts + hist, :]

    lbl = lbl_ref[...]
    lbe = jnp.exp(lbl - jnp.max(lbl, axis=0, keepdims=True))
    lb_all = lbe[0:1, :] / jnp.sum(lbe, axis=0, keepdims=True)

    c = HG_CHUNK
    nmat = n_ref[...]
    row = lax.broadcasted_iota(jnp.int32, (c, HG_DIM), 0)
    row_c = lax.broadcasted_iota(jnp.int32, (c, c), 0)
    col_c = lax.broadcasted_iota(jnp.int32, (c, c), 1)

    def chunk_body(ci, carry):
        rows = pl.ds(pl.multiple_of(ci * c, c), c)
        for h in range(HG_HEADS):
            hc = slice(h * HG_DIM, (h + 1) * HG_DIM)

            def seg(off):
                return proj_ref[rows, off + h * HG_DIM:off + (h + 1) * HG_DIM].astype(F32)

            o, st_new = _hgrn_head_chunk(seg(OFF_Q), seg(OFF_F), seg(OFF_I), lb_all[:, hc], nmat,
                                         st_ref[h], row, row_c, col_c)
            st_ref[h] = st_new
            g = seg(OFF_G)
            yb = _rms(o) * hgn_ref[:, hc] * (g * jax.nn.sigmoid(g))
            merged = jax.nn.sigmoid(seg(OFF_GA)) * ya_ref[rows, hc] + jax.nn.sigmoid(seg(OFF_GB)) * yb
            o_ref[rows, hc] = merged.astype(o_ref.dtype)
        return carry

    lax.fori_loop(0, ts // c, chunk_body, 0)


def _mixer(proj, lb_logits, hg_norm, pool_w, pool_scale, nmat, batch, seq, ts):
    tokens = proj.shape[0]
    per_batch = seq // ts
    return pl.pallas_call(
        functools.partial(_mixer_kernel, ts=ts),
        out_shape=jax.ShapeDtypeStruct((tokens, D_MODEL), BF16),
        grid=(batch, per_batch),
        in_specs=[pl.BlockSpec((ts, IN_WIDTH), lambda b, s: (b * per_batch + s, 0)),
                  pl.BlockSpec(lb_logits.shape, lambda b, s: (0, 0)),
                  pl.BlockSpec((1, D_MODEL), lambda b, s: (0, 0)),
                  pl.BlockSpec(pool_w.shape, lambda b, s: (0, 0, 0)),
                  pl.BlockSpec((1, D_MODEL), lambda b, s: (0, 0)),
                  pl.BlockSpec(nmat.shape, lambda b, s: (0, 0))],
        out_specs=pl.BlockSpec((ts, D_MODEL), lambda b, s: (b * per_batch + s, 0)),
        scratch_shapes=[pltpu.VMEM((HG_HEADS, HG_DIM, HG_DIM), F32),
                        pltpu.VMEM((POOL_HISTORY + ts, POOL_WIDTH), F32),
                        pltpu.VMEM((ts, D_MODEL), F32)],
        compiler_params=_params(("arbitrary", "arbitrary")),
        name="mixer",
    )(proj, lb_logits, hg_norm, pool_w, pool_scale, nmat)


def _out_kernel(m_ref, x_ref, ada_ref, wo_ref, g_ref, wq_ref, keys_ref, x1_ref, h2_ref, sc_ref):
    y = jnp.dot(m_ref[...], wo_ref[...], preferred_element_type=F32)
    x1 = x_ref[...] + (1.0 + ada_ref[2:3, :]) * y
    x1_ref[...] = x1
    h2 = ((_rms(x1) * g_ref[...]) * (1.0 + ada_ref[4:5, :]) + ada_ref[3:4, :]).astype(BF16)
    h2_ref[...] = h2
    q = jnp.dot(h2, wq_ref[...], preferred_element_type=F32)
    for hp in range(2 * PEER_HEADS):
        rows = slice(hp * PEER_NKEYS, (hp + 1) * PEER_NKEYS)
        qs = q[:, hp * LANES:(hp + 1) * LANES].astype(BF16)
        sc_ref[rows, :] = lax.dot_general(keys_ref[hp], qs, NT_DIMS, preferred_element_type=F32)


def _out(merged, x2d, ada3, w_out, gain, w_q, keys, seq, tm):
    tokens = x2d.shape[0]
    per_batch = seq // tm
    nsc = 2 * PEER_HEADS * PEER_NKEYS
    return pl.pallas_call(
        _out_kernel,
        out_shape=(jax.ShapeDtypeStruct((tokens, D_MODEL), F32),
                   jax.ShapeDtypeStruct((tokens, D_MODEL), BF16),
                   jax.ShapeDtypeStruct((nsc, tokens), F32)),
        grid=(tokens // tm,),
        in_specs=[pl.BlockSpec((tm, D_MODEL), lambda i: (i, 0)),
                  pl.BlockSpec((tm, D_MODEL), lambda i: (i, 0)),
                  pl.BlockSpec((None, 6, D_MODEL), lambda i: (i // per_batch, 0, 0)),
                  pl.BlockSpec((D_MODEL, D_MODEL), lambda i: (0, 0)),
                  pl.BlockSpec((1, D_MODEL), lambda i: (0, 0)),
                  pl.BlockSpec(w_q.shape, lambda i: (0, 0)),
                  pl.BlockSpec(keys.shape, lambda i: (0, 0, 0))],
        out_specs=(pl.BlockSpec((tm, D_MODEL), lambda i: (i, 0)),
                   pl.BlockSpec((tm, D_MODEL), lambda i: (i, 0)),
                   pl.BlockSpec((nsc, tm), lambda i: (0, i))),
        compiler_params=_params(("arbitrary",)),
        name="out",
    )(merged, x2d, ada3, w_out, gain, w_q, keys)


def _top16(s, row_f):
    rank = jnp.full(s.shape, float(PEER_TOPK), F32)
    vals = []
    for k in range(PEER_TOPK):
        m = jnp.max(s, axis=0, keepdims=True)
        first = jnp.min(jnp.where(s == m, row_f, float(PEER_NKEYS)), axis=0, keepdims=True)
        hit = row_f == first
        rank = jnp.where(hit, float(k), rank)
        s = jnp.where(hit, -jnp.inf, s)
        vals.append(m)
    return vals, rank


def _route_kernel(sc_ref, r1_ref, b_ref, n_ref, a_ref):
    tl = sc_ref.shape[1]
    row_f = lax.broadcasted_iota(jnp.int32, (PEER_NKEYS, tl), 0).astype(F32)
    for h in range(PEER_HEADS):
        s0 = sc_ref[(2 * h) * PEER_NKEYS:(2 * h + 1) * PEER_NKEYS, :]
        s1 = sc_ref[(2 * h + 1) * PEER_NKEYS:(2 * h + 2) * PEER_NKEYS, :]
        v0, rank0 = _top16(s0, row_f)
        v1, rank1 = _top16(s1, row_f)
        v0s = jnp.concatenate(v0, axis=0)
        v1s = jnp.concatenate(v1, axis=0)
        cands = [v0[0] + v1s[0:8], v0[0] + v1s[8:16]]
        cands += [v0[a] + v1s[0:8] for a in range(1, 8)]
        cands += [v0s[8:16] + v1[0]]
        work = list(cands)
        tau = jnp.full((1, tl), -jnp.inf, F32)
        removed = jnp.zeros((1, tl), F32)
        for _ in range(PEER_TOPK):
            m = functools.reduce(jnp.maximum, work)
            m = jnp.max(m, axis=0, keepdims=True)
            hits = [w == m for w in work]
            cnt = functools.reduce(jnp.add, [jnp.where(hh, 1.0, 0.0) for hh in hits])
            cnt = jnp.sum(cnt, axis=0, keepdims=True)
            work = [jnp.where(hh, -jnp.inf, w) for hh, w in zip(hits, work)]
            tau = jnp.where((removed < PEER_TOPK) & (removed + cnt >= PEER_TOPK), m, tau)
            removed = removed + cnt
        top = v0[0] + v1[0]
        sel = [cc >= tau for cc in cands]
        z = functools.reduce(jnp.add, [jnp.where(ss, jnp.exp(cc - top), 0.0) for ss, cc in zip(sel, cands)])
        z = jnp.sum(z, axis=0, keepdims=True)
        cnts = [jnp.where(ss, 1.0, 0.0) for ss in sel]
        n_rows = [jnp.sum(cnts[0] + cnts[1], axis=0, keepdims=True)]
        n_rows += [jnp.sum(cnts[a + 1], axis=0, keepdims=True) for a in range(1, 8)]
        n_rows += [cnts[9][a:a + 1, :] for a in range(8)]
        n_full = jnp.zeros((PEER_NKEYS, tl), F32)
        for a in range(PEER_TOPK):
            n_full = jnp.where(rank0 == float(a), n_rows[a], n_full)
        r1_ref[h] = rank1.astype(r1_ref.dtype)
        b_ref[h] = (jnp.exp(s1 - v1[0]) / z).astype(b_ref.dtype)
        n_ref[h] = n_full
        a_ref[h] = jnp.exp(s0 - v0[0])


def _route(scores_t, tl):
    tokens = scores_t.shape[1]
    shp = (PEER_HEADS, PEER_NKEYS, tokens)
    blk = pl.BlockSpec((PEER_HEADS, PEER_NKEYS, tl), lambda i: (0, 0, i))
    return pl.pallas_call(
        _route_kernel,
        out_shape=(jax.ShapeDtypeStruct(shp, BF16), jax.ShapeDtypeStruct(shp, BF16),
                   jax.ShapeDtypeStruct(shp, F32), jax.ShapeDtypeStruct(shp, F32)),
        grid=(tokens // tl,),
        in_specs=[pl.BlockSpec((scores_t.shape[0], tl), lambda i: (0, i))],
        out_specs=(blk, blk, blk, blk),
        compiler_params=_params(("arbitrary",)),
        name="route",
    )(scores_t)


def _peer_kernel(h2_ref, u_ref, vt_ref, r1_ref, b_ref, n_ref, a_ref, x1_ref, ada_ref, fg_ref, o_ref,
                 hid_ref, p_ref, acc_ref, *, tb, eb):
    e = pl.program_id(1)
    sub = 16

    @pl.when(e == 0)
    def _():
        acc_ref[...] = jnp.zeros_like(acc_ref)

    hid_ref[...] = lax.dot_general(u_ref[...], h2_ref[...], NT_DIMS, preferred_element_type=F32)

    for ii in range(eb // PEER_NKEYS):
        for lc in range(tb // LANES):
            lanes = slice(lc * LANES, (lc + 1) * LANES)
            n_h = [jnp.broadcast_to(n_ref[h, ii:ii + 1, lanes], (sub, LANES)).astype(BF16)
                   for h in range(PEER_HEADS)]
            a_h = [jnp.broadcast_to(a_ref[h, ii:ii + 1, lanes], (sub, LANES)).astype(BF16)
                   for h in range(PEER_HEADS)]
            for g in range(PEER_NKEYS // sub):
                jr = slice(g * sub, (g + 1) * sub)
                rr = slice(ii * PEER_NKEYS + g * sub, ii * PEER_NKEYS + (g + 1) * sub)
                gate = None
                for h in range(PEER_HEADS):
                    term = jnp.where(r1_ref[h, jr, lanes] < n_h[h], b_ref[h, jr, lanes],
                                     jnp.zeros((), BF16)) * a_h[h]
                    gate = term if gate is None else gate + term
                hid = hid_ref[rr, lanes]
                act = 0.5 * hid * (1.0 + lax.erf(hid * INV_SQRT2))
                p_ref[rr, lanes] = gate * act.astype(BF16)
    acc_ref[...] += jnp.dot(vt_ref[...], p_ref[...], preferred_element_type=F32)

    @pl.when(e == pl.num_programs(1) - 1)
    def _():
        x2 = x1_ref[...] + (1.0 + ada_ref[5:6, :]) * acc_ref[...].T
        o_ref[...] = _rms(x2) * fg_ref[...]


def _peer(h2, u, vt, r1t, bt, nt, at, x1, ada3, final_gain, seq, tb, eb):
    tokens = h2.shape[0]
    per_batch = seq // tb
    keys_per_tile = eb // PEER_NKEYS
    rt_blk = pl.BlockSpec((PEER_HEADS, PEER_NKEYS, tb), lambda i, e: (0, 0, i))
    key_blk = pl.BlockSpec((PEER_HEADS, keys_per_tile, tb), lambda i, e: (0, e, i))
    return pl.pallas_call(
        functools.partial(_peer_kernel, tb=tb, eb=eb),
        out_shape=jax.ShapeDtypeStruct((tokens, D_MODEL), F32),
        grid=(tokens // tb, PEER_EXPERTS // eb),
        in_specs=[pl.BlockSpec((tb, D_MODEL), lambda i, e: (i, 0)),
                  pl.BlockSpec((eb, D_MODEL), lambda i, e: (e, 0)),
                  pl.BlockSpec((D_MODEL, eb), lambda i, e: (0, e)),
                  rt_blk, rt_blk, key_blk, key_blk,
                  pl.BlockSpec((tb, D_MODEL), lambda i, e: (i, 0)),
                  pl.BlockSpec((None, 6, D_MODEL), lambda i, e: (i // per_batch, 0, 0)),
                  pl.BlockSpec((1, D_MODEL), lambda i, e: (0, 0))],
        out_specs=pl.BlockSpec((tb, D_MODEL), lambda i, e: (i, 0)),
        scratch_shapes=[pltpu.VMEM((eb, tb), F32),
                        pltpu.VMEM((eb, tb), BF16),
                        pltpu.VMEM((D_MODEL, tb), F32)],
        compiler_params=_params(("arbitrary", "arbitrary")),
        name="peer",
    )(h2, u, vt, r1t, bt, nt, at, x1, ada3, final_gain)


def kernel(x, c, w_ada, b_ada, norm1, w_in, pool_w, pool_scale, lb_logits, hg_norm, w_out,
           norm2, peer_wq, peer_keys, peer_u, peer_v, final_norm):
    batch, seq, d = x.shape
    assert d == D_MODEL and w_ada.shape[0] == 1 and lb_logits.shape[0] == 2
    assert seq % HG_CHUNK == 0 and seq >= POOL_HISTORY
    t = _tiles(batch, seq)
    x2d = x.reshape(batch * seq, d)
    row = lambda v: v.reshape(1, -1).astype(F32)

    ada3 = _ada(c, w_ada[0], b_ada[0].reshape(1, -1), t["ada_tn"]).reshape(batch, 6, d)
    proj = _proj(x2d, ada3, row(norm1[0]), w_in[0].astype(BF16), seq, t["proj_tm"], t["proj_tn"])
    merged = _mixer(proj, lb_logits, row(hg_norm[0]), pool_w[0].astype(BF16), row(pool_scale[0]),
                    jnp.asarray(_decay_sum_matrix(), BF16), batch, seq, t["mix_ts"])
    keys = peer_keys[0].reshape(2 * PEER_HEADS, PEER_NKEYS, -1).astype(BF16)
    x1, h2, scores_t = _out(merged, x2d, ada3, w_out[0].astype(BF16), row(norm2[0]),
                            peer_wq[0].astype(BF16), keys, seq, t["out_tm"])
    r1t, bt, nt, at = _route(scores_t, t["route_tl"])
    y = _peer(h2, peer_u[0].astype(BF16), peer_v[0].T.astype(BF16), r1t, bt, nt, at, x1, ada3,
              row(final_norm), seq, t["peer_tb"], t["peer_eb"])
    return y.reshape(batch, seq, d)
```

```python
import functools

import numpy as np
import jax
import jax.numpy as jnp
from jax import lax
from jax.experimental import pallas as pl
from jax.experimental.pallas import tpu as pltpu

F32 = jnp.float32
BF16 = jnp.bfloat16
EPS = 1e-6

D_MODEL = 2048
LANES = 128
POOL_WINDOWS = (2, 4, 8, 16)
POOL_WIDTH = D_MODEL // 2
POOL_GROUP_DIM = POOL_WIDTH // len(POOL_WINDOWS)
POOL_OUT_DIM = D_MODEL // len(POOL_WINDOWS)
POOL_HISTORY = 16
HG_HEADS = 16
HG_DIM = 128
HG_CHUNK = 64
HG_LEVELS = (32, 16, 8, 4, 2, 1)
PEER_HEADS = 8
PEER_NKEYS = 128
PEER_TOPK = 16
PEER_EXPERTS = PEER_NKEYS * PEER_NKEYS
INV_SQRT2 = 0.7071067811865476

OFF_POOL = 0
OFF_Q = OFF_POOL + POOL_WIDTH
OFF_F = OFF_Q + D_MODEL
OFF_I = OFF_F + D_MODEL
OFF_G = OFF_I + D_MODEL
OFF_GA = OFF_G + D_MODEL
OFF_GB = OFF_GA + D_MODEL
IN_WIDTH = OFF_GB + D_MODEL

VMEM_LIMIT_BYTES = 58 * 1024 * 1024

NT_DIMS = (((1,), (1,)), ((), ()))
TN_DIMS = (((0,), (0,)), ((), ()))


def _tiles(batch, seq):
    tokens = batch * seq
    return dict(
        ada_tn=1024,
        proj_tm=min(1024, seq), proj_tn=1024,
        mix_ts=min(256, seq),
        out_tm=min(256, seq),
        route_tl=LANES,
        peer_tb=min(512, seq), peer_eb=1024,
        tokens=tokens,
    )


def _params(semantics):
    return pltpu.CompilerParams(dimension_semantics=semantics, vmem_limit_bytes=VMEM_LIMIT_BYTES)


def _rms(x):
    return x * lax.rsqrt(jnp.mean(x * x, axis=-1, keepdims=True) + EPS)


def _ada_kernel(c_ref, w_ref, b_ref, o_ref):
    c = c_ref[...]
    cs = c * jax.nn.sigmoid(c)
    o_ref[...] = jnp.dot(cs.astype(BF16), w_ref[...].astype(BF16), preferred_element_type=F32) + b_ref[...]


def _ada(c, w, b, tn):
    bsz, n = c.shape[0], w.shape[1]
    return pl.pallas_call(
        _ada_kernel,
        out_shape=jax.ShapeDtypeStruct((bsz, n), F32),
        grid=(n // tn,),
        in_specs=[pl.BlockSpec((bsz, D_MODEL), lambda j: (0, 0)),
                  pl.BlockSpec((D_MODEL, tn), lambda j: (0, j)),
                  pl.BlockSpec((1, tn), lambda j: (0, j))],
        out_specs=pl.BlockSpec((bsz, tn), lambda j: (0, j)),
        compiler_params=_params(("arbitrary",)),
        name="ada",
    )(c, w, b)


def _proj_kernel(x_ref, ada_ref, g_ref, w_ref, o_ref, h_ref):
    @pl.when(pl.program_id(1) == 0)
    def _():
        y = _rms(x_ref[...]) * g_ref[...]
        h_ref[...] = (y * (1.0 + ada_ref[1:2, :]) + ada_ref[0:1, :]).astype(BF16)

    o_ref[...] = jnp.dot(h_ref[...], w_ref[...], preferred_element_type=F32).astype(o_ref.dtype)


def _proj(x2d, ada3, gain, w, seq, tm, tn):
    tokens, n = x2d.shape[0], w.shape[1]
    per_batch = seq // tm
    return pl.pallas_call(
        _proj_kernel,
        out_shape=jax.ShapeDtypeStruct((tokens, n), BF16),
        grid=(tokens // tm, n // tn),
        in_specs=[pl.BlockSpec((tm, D_MODEL), lambda i, j: (i, 0)),
                  pl.BlockSpec((None, 6, D_MODEL), lambda i, j: (i // per_batch, 0, 0)),
                  pl.BlockSpec((1, D_MODEL), lambda i, j: (0, 0)),
                  pl.BlockSpec((D_MODEL, tn), lambda i, j: (0, j))],
        out_specs=pl.BlockSpec((tm, tn), lambda i, j: (i, j)),
        scratch_shapes=[pltpu.VMEM((tm, D_MODEL), BF16)],
        compiler_params=_params(("arbitrary", "arbitrary")),
        name="proj",
    )(x2d, ada3, gain, w)


def _decay_sum_matrix():
    c = HG_CHUNK
    groups = []
    for m in HG_LEVELS:
        g = np.zeros((c, c), np.float32)
        for t in range(c):
            ref = (t // (2 * m)) * 2 * m + m - 1
            if (t // m) % 2 == 1:
                g[t, ref + 1:t + 1] = 1.0
            else:
                g[t, t + 1:ref + 1] = 1.0
        groups.append(g)
    groups.append(np.tril(np.ones((c, c), np.float32)))
    groups.append(np.triu(np.ones((c, c), np.float32), 1))
    n = np.concatenate(groups, axis=0)
    return np.concatenate([n, n, n], axis=1)


def _hgrn_head_chunk(q, z, iv, lb, nmat, st, row, row_c, col_c):
    c = HG_CHUNK
    t = jnp.exp(-jnp.abs(z))
    r = 1.0 / (1.0 + t)
    pos = z >= 0.0
    sig_pos = jnp.where(pos, r, t * r)
    sig_neg = jnp.where(pos, t * r, r)
    lf = jnp.log(lb + (1.0 - lb) * sig_pos)
    kk = (1.0 - lb) * sig_neg
    hi = lf.astype(BF16)
    rem = lf - hi.astype(F32)
    mid = rem.astype(BF16)
    lo = (rem - mid.astype(F32)).astype(BF16)
    dec = jnp.exp(jnp.dot(nmat, jnp.concatenate([hi, mid, lo], axis=0), preferred_element_type=F32))

    attn = jnp.zeros((c, c), F32)
    for li, m in enumerate(HG_LEVELS):
        dl = dec[li * c:(li + 1) * c]
        right = (row & m) != 0
        xs = jnp.where(right, q, kk) * dl
        ql = jnp.where(right, xs, 0.0).astype(BF16)
        kl = jnp.where(right, 0.0, xs).astype(BF16)
        a = lax.dot_general(ql, kl, NT_DIMS, preferred_element_type=F32)
        if 2 * m < c:
            sh = int(np.log2(2 * m))
            a = jnp.where((row_c >> sh) == (col_c >> sh), a, 0.0)
        attn = attn + a
    dc = dec[6 * c:7 * c]
    dr = dec[7 * c:8 * c]
    diag = jnp.sum(q * kk, axis=-1, keepdims=True)
    ivb = iv.astype(BF16)
    o = (jnp.dot(attn.astype(BF16), ivb, preferred_element_type=F32) + diag * iv
         + lax.dot_general((q * dc).astype(BF16), st.astype(BF16), NT_DIMS, preferred_element_type=F32))
    upd = lax.dot_general(ivb, (kk * dr).astype(BF16), TN_DIMS, preferred_element_type=F32)
    return o, st * dc[c - 1:c, :] + upd


def _mixer_kernel(proj_ref, lbl_ref, hgn_ref, pw_ref, ps_ref, n_ref, o_ref, st_ref, pbuf_ref, ya_ref, *, ts):
    sblk = pl.program_id(1)
    hist = POOL_HISTORY

    @pl.when(sblk == 0)
    def _():
        st_ref[...] = jnp.zeros_like(st_ref)
        pbuf_ref[0:hist, :] = jnp.zeros((hist, POOL_WIDTH), F32)

    pbuf_ref[hist:hist + ts, :] = proj_ref[:, OFF_POOL:OFF_POOL + POOL_WIDTH].astype(F32)
    pos1 = (sblk * ts + lax.broadcasted_iota(jnp.int32, (ts, 1), 0) + 1).astype(F32)
    for g, w in enumerate(POOL_WINDOWS):
        cols = slice(g * POOL_GROUP_DIM, (g + 1) * POOL_GROUP_DIM)
        cur = pbuf_ref[hist:hist + ts, cols]
        acc = cur
        for back in range(1, w):
            acc = acc + pbuf_ref[hist - back:hist - back + ts, cols]
        pooled = acc / jnp.minimum(pos1, float(w)) - cur
        oc = slice(g * POOL_OUT_DIM, (g + 1) * POOL_OUT_DIM)
        ya_ref[:, oc] = jnp.dot(pooled.astype(BF16), pw_ref[g], preferred_element_type=F32) * ps_ref[:, oc]
    pbuf_ref[0:hist, :] = pbuf_ref[ts:ts + hist, :]

    lbl = lbl_ref[...]
    lbe = jnp.exp(lbl - jnp.max(lbl, axis=0, keepdims=True))
    lb_all = lbe[0:1, :] / jnp.sum(lbe, axis=0, keepdims=True)

    c = HG_CHUNK
    nmat = n_ref[...]
    row = lax.broadcasted_iota(jnp.int32, (c, HG_DIM), 0)
    row_c = lax.broadcasted_iota(jnp.int32, (c, c), 0)
    col_c = lax.broadcasted_iota(jnp.int32, (c, c), 1)

    def chunk_body(ci, carry):
        rows = pl.ds(pl.multiple_of(ci * c, c), c)
        for h in range(HG_HEADS):
            hc = slice(h * HG_DIM, (h + 1) * HG_DIM)

            def seg(off):
                return proj_ref[rows, off + h * HG_DIM:off + (h + 1) * HG_DIM].astype(F32)

            o, st_new = _hgrn_head_chunk(seg(OFF_Q), seg(OFF_F), seg(OFF_I), lb_all[:, hc], nmat,
                                         st_ref[h], row, row_c, col_c)
            st_ref[h] = st_new
            g = seg(OFF_G)
            yb = _rms(o) * hgn_ref[:, hc] * (g * jax.nn.sigmoid(g))
            merged = jax.nn.sigmoid(seg(OFF_GA)) * ya_ref[rows, hc] + jax.nn.sigmoid(seg(OFF_GB)) * yb
            o_ref[rows, hc] = merged.astype(o_ref.dtype)
        return carry

    lax.fori_loop(0, ts // c, chunk_body, 0)


def _mixer(proj, lb_logits, hg_norm, pool_w, pool_scale, nmat, batch, seq, ts):
    tokens = proj.shape[0]
    per_batch = seq // ts
    return pl.pallas_call(
        functools.partial(_mixer_kernel, ts=ts),
        out_shape=jax.ShapeDtypeStruct((tokens, D_MODEL), BF16),
        grid=(batch, per_batch),
        in_specs=[pl.BlockSpec((ts, IN_WIDTH), lambda b, s: (b * per_batch + s, 0)),
                  pl.BlockSpec(lb_logits.shape, lambda b, s: (0, 0)),
                  pl.BlockSpec((1, D_MODEL), lambda b, s: (0, 0)),
                  pl.BlockSpec(pool_w.shape, lambda b, s: (0, 0, 0)),
                  pl.BlockSpec((1, D_MODEL), lambda b, s: (0, 0)),
                  pl.BlockSpec(nmat.shape, lambda b, s: (0, 0))],
        out_specs=pl.BlockSpec((ts, D_MODEL), lambda b, s: (b * per_batch + s, 0)),
        scratch_shapes=[pltpu.VMEM((HG_HEADS, HG_DIM, HG_DIM), F32),
                        pltpu.VMEM((POOL_HISTORY + ts, POOL_WIDTH), F32),
                        pltpu.VMEM((ts, D_MODEL), F32)],
        compiler_params=_params(("arbitrary", "arbitrary")),
        name="mixer",
    )(proj, lb_logits, hg_norm, pool_w, pool_scale, nmat)


def _out_kernel(m_ref, x_ref, ada_ref, wo_ref, g_ref, wq_ref, keys_ref, x1_ref, h2_ref, sc_ref):
    y = jnp.dot(m_ref[...], wo_ref[...], preferred_element_type=F32)
    x1 = x_ref[...] + (1.0 + ada_ref[2:3, :]) * y
    x1_ref[...] = x1
    h2 = ((_rms(x1) * g_ref[...]) * (1.0 + ada_ref[4:5, :]) + ada_ref[3:4, :]).astype(BF16)
    h2_ref[...] = h2
    q = jnp.dot(h2, wq_ref[...], preferred_element_type=F32)
    for hp in range(2 * PEER_HEADS):
        rows = slice(hp * PEER_NKEYS, (hp + 1) * PEER_NKEYS)
        qs = q[:, hp * LANES:(hp + 1) * LANES].astype(BF16)
        sc_ref[rows, :] = lax.dot_general(keys_ref[hp], qs, NT_DIMS, preferred_element_type=F32)


def _out(merged, x2d, ada3, w_out, gain, w_q, keys, seq, tm):
    tokens = x2d.shape[0]
    per_batch = seq // tm
    nsc = 2 * PEER_HEADS * PEER_NKEYS
    return pl.pallas_call(
        _out_kernel,
        out_shape=(jax.ShapeDtypeStruct((tokens, D_MODEL), F32),
                   jax.ShapeDtypeStruct((tokens, D_MODEL), BF16),
                   jax.ShapeDtypeStruct((nsc, tokens), F32)),
        grid=(tokens // tm,),
        in_specs=[pl.BlockSpec((tm, D_MODEL), lambda i: (i, 0)),
                  pl.BlockSpec((tm, D_MODEL), lambda i: (i, 0)),
                  pl.BlockSpec((None, 6, D_MODEL), lambda i: (i // per_batch, 0, 0)),
                  pl.BlockSpec((D_MODEL, D_MODEL), lambda i: (0, 0)),
                  pl.BlockSpec((1, D_MODEL), lambda i: (0, 0)),
                  pl.BlockSpec(w_q.shape, lambda i: (0, 0)),
                  pl.BlockSpec(keys.shape, lambda i: (0, 0, 0))],
        out_specs=(pl.BlockSpec((tm, D_MODEL), lambda i: (i, 0)),
                   pl.BlockSpec((tm, D_MODEL), lambda i: (i, 0)),
                   pl.BlockSpec((nsc, tm), lambda i: (0, i))),
        compiler_params=_params(("arbitrary",)),
        name="out",
    )(merged, x2d, ada3, w_out, gain, w_q, keys)


def _top16(s, row_f):
    rank = jnp.full(s.shape, float(PEER_TOPK), F32)
    vals = []
    for k in range(PEER_TOPK):
        m = jnp.max(s, axis=0, keepdims=True)
        first = jnp.min(jnp.where(s == m, row_f, float(PEER_NKEYS)), axis=0, keepdims=True)
        hit = row_f == first
        rank = jnp.where(hit, float(k), rank)
        s = jnp.where(hit, -jnp.inf, s)
        vals.append(m)
    return vals, rank


def _route_kernel(sc_ref, r1_ref, b_ref, n_ref, a_ref):
    tl = sc_ref.shape[1]
    row_f = lax.broadcasted_iota(jnp.int32, (PEER_NKEYS, tl), 0).astype(F32)
    for h in range(PEER_HEADS):
        s0 = sc_ref[(2 * h) * PEER_NKEYS:(2 * h + 1) * PEER_NKEYS, :]
        s1 = sc_ref[(2 * h + 1) * PEER_NKEYS:(2 * h + 2) * PEER_NKEYS, :]
        v0, rank0 = _top16(s0, row_f)
        v1, rank1 = _top16(s1, row_f)
        v0s = jnp.concatenate(v0, axis=0)
        v1s = jnp.concatenate(v1, axis=0)
        cands = [v0[0] + v1s[0:8], v0[0] + v1s[8:16]]
        cands += [v0[a] + v1s[0:8] for a in range(1, 8)]
        cands += [v0s[8:16] + v1[0]]
        work = list(cands)
        tau = jnp.full((1, tl), -jnp.inf, F32)
        removed = jnp.zeros((1, tl), F32)
        for _ in range(PEER_TOPK):
            m = functools.reduce(jnp.maximum, work)
            m = jnp.max(m, axis=0, keepdims=True)
            hits = [w == m for w in work]
            cnt = functools.reduce(jnp.add, [jnp.where(hh, 1.0, 0.0) for hh in hits])
            cnt = jnp.sum(cnt, axis=0, keepdims=True)
            work = [jnp.where(hh, -jnp.inf, w) for hh, w in zip(hits, work)]
            tau = jnp.where((removed < PEER_TOPK) & (removed + cnt >= PEER_TOPK), m, tau)
            removed = removed + cnt
        top = v0[0] + v1[0]
        sel = [cc >= tau for cc in cands]
        z = functools.reduce(jnp.add, [jnp.where(ss, jnp.exp(cc - top), 0.0) for ss, cc in zip(sel, cands)])
        z = jnp.sum(z, axis=0, keepdims=True)
        cnts = [jnp.where(ss, 1.0, 0.0) for ss in sel]
        n_rows = [jnp.sum(cnts[0] + cnts[1], axis=0, keepdims=True)]
        n_rows += [jnp.sum(cnts[a + 1], axis=0, keepdims=True) for a in range(1, 8)]
        n_rows += [cnts[9][a:a + 1, :] for a in range(8)]
        n_full = jnp.zeros((PEER_NKEYS, tl), F32)
        for a in range(PEER_TOPK):
            n_full = jnp.where(rank0 == float(a), n_rows[a], n_full)
        r1_ref[h] = rank1.astype(r1_ref.dtype)
        b_ref[h] = (jnp.exp(s1 - v1[0]) / z).astype(b_ref.dtype)
        n_ref[h] = n_full
        a_ref[h] = jnp.exp(s0 - v0[0])


def _route(scores_t, tl):
    tokens = scores_t.shape[1]
    shp = (PEER_HEADS, PEER_NKEYS, tokens)
    blk = pl.BlockSpec((PEER_HEADS, PEER_NKEYS, tl), lambda i: (0, 0, i))
    return pl.pallas_call(
        _route_kernel,
        out_shape=(jax.ShapeDtypeStruct(shp, BF16), jax.ShapeDtypeStruct(shp, BF16),
                   jax.ShapeDtypeStruct(shp, F32), jax.ShapeDtypeStruct(shp, F32)),
        grid=(tokens // tl,),
        in_specs=[pl.BlockSpec((scores_t.shape[0], tl), lambda i: (0, i))],
        out_specs=(blk, blk, blk, blk),
        compiler_params=_params(("arbitrary",)),
        name="route",
    )(scores_t)


def _peer_stages(u_ref, h2_ref, vt_ref, r1_ref, b_ref, n_ref, a_ref, acc_ref,
                 hid_w, hid_r, p_w, p_r, *, tb, eb):
    sub = 16
    mxu_n = 256
    m_rows = 512

    def hidden_piece(mh, nh):
        rows = slice(mh * m_rows, (mh + 1) * m_rows)
        toks = slice(nh * mxu_n, (nh + 1) * mxu_n)
        hid_w[rows, toks] = lax.dot_general(u_ref[rows, :], h2_ref[toks, :], NT_DIMS,
                                            preferred_element_type=F32)

    def gate_piece(ii, lc):
        lanes = slice(lc * LANES, (lc + 1) * LANES)
        n_h = [jnp.broadcast_to(n_ref[h, ii:ii + 1, lanes], (sub, LANES)).astype(BF16)
               for h in range(PEER_HEADS)]
        a_h = [jnp.broadcast_to(a_ref[h, ii:ii + 1, lanes], (sub, LANES)).astype(BF16)
               for h in range(PEER_HEADS)]
        for g in range(PEER_NKEYS // sub):
            jr = slice(g * sub, (g + 1) * sub)
            rr = slice(ii * PEER_NKEYS + g * sub, ii * PEER_NKEYS + (g + 1) * sub)
            gate = None
            for h in range(PEER_HEADS):
                term = jnp.where(r1_ref[h, jr, lanes] < n_h[h], b_ref[h, jr, lanes],
                                 jnp.zeros((), BF16)) * a_h[h]
                gate = term if gate is None else gate + term
            hid = hid_r[rr, lanes]
            act = 0.5 * hid * (1.0 + lax.erf(hid * INV_SQRT2))
            p_w[rr, lanes] = gate * act.astype(BF16)

    def accumulate_piece(mq, nh):
        rows = slice(mq * m_rows, (mq + 1) * m_rows)
        toks = slice(nh * mxu_n, (nh + 1) * mxu_n)
        acc_ref[rows, toks] += jnp.dot(vt_ref[rows, :], p_r[:, toks], preferred_element_type=F32)

    hidden = [(mh, nh) for nh in range(tb // mxu_n) for mh in range(eb // m_rows)]
    gates = [(ii, lc) for ii in range(eb // PEER_NKEYS) for lc in range(tb // LANES)]
    accs = [(mq, nh) for nh in range(tb // mxu_n) for mq in range(D_MODEL // m_rows)]
    g_per = -(-len(gates) // (len(hidden) + len(accs)))
    a_per = len(accs) // len(hidden)
    gi = ai = 0
    for hp in hidden:
        hidden_piece(*hp)
        for ap in accs[ai:ai + a_per]:
            for gp in gates[gi:gi + g_per]:
                gate_piece(*gp)
            gi += g_per
            accumulate_piece(*ap)
        ai += a_per
        for gp in gates[gi:gi + g_per]:
            gate_piece(*gp)
        gi += g_per
    for gp in gates[gi:]:
        gate_piece(*gp)
    for ap in accs[ai:]:
        accumulate_piece(*ap)


def _peer_kernel(h2_ref, u_ref, vt_ref, r1_ref, b_ref, n_ref, a_ref, x1_ref, ada_ref, fg_ref, o_ref,
                 hid0_ref, hid1_ref, p0_ref, p1_ref, acc_ref, *, tb, eb, n_tiles, n_work):
    s = pl.program_id(0)
    e3 = jnp.clip(s - 2, 0, n_work - 1) % n_tiles

    @pl.when(s == 0)
    def _():
        hid1_ref[...] = jnp.zeros_like(hid1_ref)
        p0_ref[...] = jnp.zeros_like(p0_ref)

    @pl.when(e3 == 0)
    def _():
        acc_ref[...] = jnp.zeros_like(acc_ref)

    stages = functools.partial(_peer_stages, u_ref, h2_ref, vt_ref, r1_ref, b_ref, n_ref, a_ref, acc_ref,
                               tb=tb, eb=eb)

    @pl.when(s % 2 == 0)
    def _():
        stages(hid0_ref, hid1_ref, p1_ref, p0_ref)

    @pl.when(s % 2 == 1)
    def _():
        stages(hid1_ref, hid0_ref, p0_ref, p1_ref)

    @pl.when((e3 == n_tiles - 1) & (s >= 2))
    def _():
        x2 = x1_ref[...] + (1.0 + ada_ref[5:6, :]) * acc_ref[...].T
        o_ref[...] = _rms(x2) * fg_ref[...]


def _peer(h2, u, vt, r1t, bt, nt, at, x1, ada3, final_gain, seq, tb, eb):
    tokens = h2.shape[0]
    per_batch = seq // tb
    n_tiles = PEER_EXPERTS // eb
    n_work = (tokens // tb) * n_tiles
    keys_per_tile = eb // PEER_NKEYS

    def stage(lag):
        def split(s):
            w = jnp.clip(s - lag, 0, n_work - 1)
            return w // n_tiles, w % n_tiles
        return split

    st1, st2, st3 = stage(0), stage(1), stage(2)
    rt_blk = pl.BlockSpec((PEER_HEADS, PEER_NKEYS, tb), lambda s: (0, 0, st2(s)[0]))
    key_blk = pl.BlockSpec((PEER_HEADS, keys_per_tile, tb), lambda s: (0, st2(s)[1], st2(s)[0]))
    return pl.pallas_call(
        functools.partial(_peer_kernel, tb=tb, eb=eb, n_tiles=n_tiles, n_work=n_work),
        out_shape=jax.ShapeDtypeStruct((tokens, D_MODEL), F32),
        grid=(n_work + 2,),
        in_specs=[pl.BlockSpec((tb, D_MODEL), lambda s: (st1(s)[0], 0)),
                  pl.BlockSpec((eb, D_MODEL), lambda s: (st1(s)[1], 0)),
                  pl.BlockSpec((D_MODEL, eb), lambda s: (0, st3(s)[1])),
                  rt_blk, rt_blk, key_blk, key_blk,
                  pl.BlockSpec((tb, D_MODEL), lambda s: (st3(s)[0], 0), pipeline_mode=pl.Buffered(1)),
                  pl.BlockSpec((None, 6, D_MODEL), lambda s: (st3(s)[0] // per_batch, 0, 0)),
                  pl.BlockSpec((1, D_MODEL), lambda s: (0, 0))],
        out_specs=pl.BlockSpec((tb, D_MODEL), lambda s: (st3(s)[0], 0)),
        scratch_shapes=[pltpu.VMEM((eb, tb), F32), pltpu.VMEM((eb, tb), F32),
                        pltpu.VMEM((eb, tb), BF16), pltpu.VMEM((eb, tb), BF16),
                        pltpu.VMEM((D_MODEL, tb), F32)],
        compiler_params=_params(("arbitrary",)),
        name="peer",
    )(h2, u, vt, r1t, bt, nt, at, x1, ada3, final_gain)


def kernel(x, c, w_ada, b_ada, norm1, w_in, pool_w, pool_scale, lb_logits, hg_norm, w_out,
           norm2, peer_wq, peer_keys, peer_u, peer_v, final_norm):
    batch, seq, d = x.shape
    assert d == D_MODEL and w_ada.shape[0] == 1 and lb_logits.shape[0] == 2
    assert seq % HG_CHUNK == 0 and seq >= POOL_HISTORY
    t = _tiles(batch, seq)
    x2d = x.reshape(batch * seq, d)
    row = lambda v: v.reshape(1, -1).astype(F32)

    ada3 = _ada(c, w_ada[0], b_ada[0].reshape(1, -1), t["ada_tn"]).reshape(batch, 6, d)
    proj = _proj(x2d, ada3, row(norm1[0]), w_in[0].astype(BF16), seq, t["proj_tm"], t["proj_tn"])
    merged = _mixer(proj, lb_logits, row(hg_norm[0]), pool_w[0].astype(BF16), row(pool_scale[0]),
                    jnp.asarray(_decay_sum_matrix(), BF16), batch, seq, t["mix_ts"])
    keys = peer_keys[0].reshape(2 * PEER_HEADS, PEER_NKEYS, -1).astype(BF16)
    x1, h2, scores_t = _out(merged, x2d, ada3, w_out[0].astype(BF16), row(norm2[0]),
                            peer_wq[0].astype(BF16), keys, seq, t["out_tm"])
    r1t, bt, nt, at = _route(scores_t, t["route_tl"])
    y = _peer(h2, peer_u[0].astype(BF16), peer_v[0].T.astype(BF16), r1t, bt, nt, at, x1, ada3,
              row(final_norm), seq, t["peer_tb"], t["peer_eb"])
    return y.reshape(batch, seq, d)
```

```python
import functools

import numpy as np
import jax
import jax.numpy as jnp
from jax import lax
from jax.experimental import pallas as pl
from jax.experimental.pallas import tpu as pltpu

F32 = jnp.float32
BF16 = jnp.bfloat16
EPS = 1e-6

D_MODEL = 2048
LANES = 128
POOL_WINDOWS = (2, 4, 8, 16)
POOL_WIDTH = D_MODEL // 2
POOL_GROUP_DIM = POOL_WIDTH // len(POOL_WINDOWS)
POOL_OUT_DIM = D_MODEL // len(POOL_WINDOWS)
POOL_HISTORY = 16
HG_HEADS = 16
HG_DIM = 128
HG_CHUNK = 64
HG_LEVELS = (32, 16, 8, 4, 2, 1)
PEER_HEADS = 8
PEER_NKEYS = 128
PEER_TOPK = 16
PEER_EXPERTS = PEER_NKEYS * PEER_NKEYS
INV_SQRT2 = 0.7071067811865476

OFF_POOL = 0
OFF_Q = OFF_POOL + POOL_WIDTH
OFF_F = OFF_Q + D_MODEL
OFF_I = OFF_F + D_MODEL
OFF_G = OFF_I + D_MODEL
OFF_GA = OFF_G + D_MODEL
OFF_GB = OFF_GA + D_MODEL
IN_WIDTH = OFF_GB + D_MODEL

VMEM_LIMIT_BYTES = 58 * 1024 * 1024

NT_DIMS = (((1,), (1,)), ((), ()))
TN_DIMS = (((0,), (0,)), ((), ()))


def _tiles(batch, seq):
    tokens = batch * seq
    return dict(
        ada_tn=1024,
        proj_tm=min(1024, seq), proj_tn=1024,
        mix_ts=min(256, seq),
        out_tm=min(256, seq),
        route_tl=LANES,
        peer_tb=min(512, seq), peer_eb=1024,
        tokens=tokens,
    )


def _params(semantics):
    return pltpu.CompilerParams(dimension_semantics=semantics, vmem_limit_bytes=VMEM_LIMIT_BYTES)


def _rms(x):
    return x * lax.rsqrt(jnp.mean(x * x, axis=-1, keepdims=True) + EPS)


def _ada_kernel(c_ref, w_ref, b_ref, o_ref):
    c = c_ref[...]
    cs = c * jax.nn.sigmoid(c)
    o_ref[...] = jnp.dot(cs.astype(BF16), w_ref[...].astype(BF16), preferred_element_type=F32) + b_ref[...]


def _ada(c, w, b, tn):
    bsz, n = c.shape[0], w.shape[1]
    return pl.pallas_call(
        _ada_kernel,
        out_shape=jax.ShapeDtypeStruct((bsz, n), F32),
        grid=(n // tn,),
        in_specs=[pl.BlockSpec((bsz, D_MODEL), lambda j: (0, 0)),
                  pl.BlockSpec((D_MODEL, tn), lambda j: (0, j)),
                  pl.BlockSpec((1, tn), lambda j: (0, j))],
        out_specs=pl.BlockSpec((bsz, tn), lambda j: (0, j)),
        compiler_params=_params(("arbitrary",)),
        name="ada",
    )(c, w, b)


def _proj_kernel(x_ref, ada_ref, g_ref, w_ref, o_ref, h_ref):
    @pl.when(pl.program_id(1) == 0)
    def _():
        y = _rms(x_ref[...]) * g_ref[...]
        h_ref[...] = (y * (1.0 + ada_ref[1:2, :]) + ada_ref[0:1, :]).astype(BF16)

    o_ref[...] = jnp.dot(h_ref[...], w_ref[...], preferred_element_type=F32).astype(o_ref.dtype)


def _proj(x2d, ada3, gain, w, seq, tm, tn):
    tokens, n = x2d.shape[0], w.shape[1]
    per_batch = seq // tm
    return pl.pallas_call(
        _proj_kernel,
        out_shape=jax.ShapeDtypeStruct((tokens, n), BF16),
        grid=(tokens // tm, n // tn),
        in_specs=[pl.BlockSpec((tm, D_MODEL), lambda i, j: (i, 0)),
                  pl.BlockSpec((None, 6, D_MODEL), lambda i, j: (i // per_batch, 0, 0)),
                  pl.BlockSpec((1, D_MODEL), lambda i, j: (0, 0)),
                  pl.BlockSpec((D_MODEL, tn), lambda i, j: (0, j))],
        out_specs=pl.BlockSpec((tm, tn), lambda i, j: (i, j)),
        scratch_shapes=[pltpu.VMEM((tm, D_MODEL), BF16)],
        compiler_params=_params(("arbitrary", "arbitrary")),
        name="proj",
    )(x2d, ada3, gain, w)


def _decay_sum_matrix():
    c = HG_CHUNK
    groups = []
    for m in HG_LEVELS:
        g = np.zeros((c, c), np.float32)
        for t in range(c):
            ref = (t // (2 * m)) * 2 * m + m - 1
            if (t // m) % 2 == 1:
                g[t, ref + 1:t + 1] = 1.0
            else:
                g[t, t + 1:ref + 1] = 1.0
        groups.append(g)
    groups.append(np.tril(np.ones((c, c), np.float32)))
    groups.append(np.triu(np.ones((c, c), np.float32), 1))
    n = np.concatenate(groups, axis=0)
    return np.concatenate([n, n, n], axis=1)


def _hgrn_head_chunk(q, z, iv, lb, nmat, st, row, row_c, col_c):
    c = HG_CHUNK
    t = jnp.exp(-jnp.abs(z))
    r = 1.0 / (1.0 + t)
    pos = z >= 0.0
    sig_pos = jnp.where(pos, r, t * r)
    sig_neg = jnp.where(pos, t * r, r)
    lf = jnp.log(lb + (1.0 - lb) * sig_pos)
    kk = (1.0 - lb) * sig_neg
    hi = lf.astype(BF16)
    rem = lf - hi.astype(F32)
    mid = rem.astype(BF16)
    lo = (rem - mid.astype(F32)).astype(BF16)
    dec = jnp.exp(jnp.dot(nmat, jnp.concatenate([hi, mid, lo], axis=0), preferred_element_type=F32))

    attn = jnp.zeros((c, c), F32)
    for li, m in enumerate(HG_LEVELS):
        dl = dec[li * c:(li + 1) * c]
        right = (row & m) != 0
        xs = jnp.where(right, q, kk) * dl
        ql = jnp.where(right, xs, 0.0).astype(BF16)
        kl = jnp.where(right, 0.0, xs).astype(BF16)
        a = lax.dot_general(ql, kl, NT_DIMS, preferred_element_type=F32)
        if 2 * m < c:
            sh = int(np.log2(2 * m))
            a = jnp.where((row_c >> sh) == (col_c >> sh), a, 0.0)
        attn = attn + a
    dc = dec[6 * c:7 * c]
    dr = dec[7 * c:8 * c]
    diag = jnp.sum(q * kk, axis=-1, keepdims=True)
    ivb = iv.astype(BF16)
    o = (jnp.dot(attn.astype(BF16), ivb, preferred_element_type=F32) + diag * iv
         + lax.dot_general((q * dc).astype(BF16), st.astype(BF16), NT_DIMS, preferred_element_type=F32))
    upd = lax.dot_general(ivb, (kk * dr).astype(BF16), TN_DIMS, preferred_element_type=F32)
    return o, st * dc[c - 1:c, :] + upd


def _mixer_kernel(proj_ref, lbl_ref, hgn_ref, pw_ref, ps_ref, n_ref, o_ref, st_ref, pbuf_ref, ya_ref, *, ts):
    sblk = pl.program_id(1)
    hist = POOL_HISTORY

    @pl.when(sblk == 0)
    def _():
        st_ref[...] = jnp.zeros_like(st_ref)
        pbuf_ref[0:hist, :] = jnp.zeros((hist, POOL_WIDTH), F32)

    pbuf_ref[hist:hist + ts, :] = proj_ref[:, OFF_POOL:OFF_POOL + POOL_WIDTH].astype(F32)
    pos1 = (sblk * ts + lax.broadcasted_iota(jnp.int32, (ts, 1), 0) + 1).astype(F32)
    for g, w in enumerate(POOL_WINDOWS):
        cols = slice(g * POOL_GROUP_DIM, (g + 1) * POOL_GROUP_DIM)
        cur = pbuf_ref[hist:hist + ts, cols]
        acc = cur
        for back in range(1, w):
            acc = acc + pbuf_ref[hist - back:hist - back + ts, cols]
        pooled = acc / jnp.minimum(pos1, float(w)) - cur
        oc = slice(g * POOL_OUT_DIM, (g + 1) * POOL_OUT_DIM)
        ya_ref[:, oc] = jnp.dot(pooled.astype(BF16), pw_ref[g], preferred_element_type=F32) * ps_ref[:, oc]
    pbuf_ref[0:hist, :] = pbuf_ref[ts:ts + hist, :]

    lbl = lbl_ref[...]
    lbe = jnp.exp(lbl - jnp.max(lbl, axis=0, keepdims=True))
    lb_all = lbe[0:1, :] / jnp.sum(lbe, axis=0, keepdims=True)

    c = HG_CHUNK
    nmat = n_ref[...]
    row = lax.broadcasted_iota(jnp.int32, (c, HG_DIM), 0)
    row_c = lax.broadcasted_iota(jnp.int32, (c, c), 0)
    col_c = lax.broadcasted_iota(jnp.int32, (c, c), 1)

    def chunk_body(ci, carry):
        rows = pl.ds(pl.multiple_of(ci * c, c), c)
        for h in range(HG_HEADS):
            hc = slice(h * HG_DIM, (h + 1) * HG_DIM)

            def seg(off):
                return proj_ref[rows, off + h * HG_DIM:off + (h + 1) * HG_DIM].astype(F32)

            o, st_new = _hgrn_head_chunk(seg(OFF_Q), seg(OFF_F), seg(OFF_I), lb_all[:, hc], nmat,
                                         st_ref[h], row, row_c, col_c)
            st_ref[h] = st_new
            g = seg(OFF_G)
            yb = _rms(o) * hgn_ref[:, hc] * (g * jax.nn.sigmoid(g))
            merged = jax.nn.sigmoid(seg(OFF_GA)) * ya_ref[rows, hc] + jax.nn.sigmoid(seg(OFF_GB)) * yb
            o_ref[rows, hc] = merged.astype(o_ref.dtype)
        return carry

    lax.fori_loop(0, ts // c, chunk_body, 0)


def _mixer(proj, lb_logits, hg_norm, pool_w, pool_scale, nmat, batch, seq, ts):
    tokens = proj.shape[0]
    per_batch = seq // ts
    return pl.pallas_call(
        functools.partial(_mixer_kernel, ts=ts),
        out_shape=jax.ShapeDtypeStruct((tokens, D_MODEL), BF16),
        grid=(batch, per_batch),
        in_specs=[pl.BlockSpec((ts, IN_WIDTH), lambda b, s: (b * per_batch + s, 0)),
                  pl.BlockSpec(lb_logits.shape, lambda b, s: (0, 0)),
                  pl.BlockSpec((1, D_MODEL), lambda b, s: (0, 0)),
                  pl.BlockSpec(pool_w.shape, lambda b, s: (0, 0, 0)),
                  pl.BlockSpec((1, D_MODEL), lambda b, s: (0, 0)),
                  pl.BlockSpec(nmat.shape, lambda b, s: (0, 0))],
        out_specs=pl.BlockSpec((ts, D_MODEL), lambda b, s: (b * per_batch + s, 0)),
        scratch_shapes=[pltpu.VMEM((HG_HEADS, HG_DIM, HG_DIM), F32),
                        pltpu.VMEM((POOL_HISTORY + ts, POOL_WIDTH), F32),
                        pltpu.VMEM((ts, D_MODEL), F32)],
        compiler_params=_params(("arbitrary", "arbitrary")),
        name="mixer",
    )(proj, lb_logits, hg_norm, pool_w, pool_scale, nmat)


def _out_kernel(m_ref, x_ref, ada_ref, wo_ref, g_ref, wq_ref, keys_ref, x1_ref, h2_ref, sc_ref):
    y = jnp.dot(m_ref[...], wo_ref[...], preferred_element_type=F32)
    x1 = x_ref[...] + (1.0 + ada_ref[2:3, :]) * y
    x1_ref[...] = x1
    h2 = ((_rms(x1) * g_ref[...]) * (1.0 + ada_ref[4:5, :]) + ada_ref[3:4, :]).astype(BF16)
    h2_ref[...] = h2
    q = jnp.dot(h2, wq_ref[...], preferred_element_type=F32)
    for hp in range(2 * PEER_HEADS):
        rows = slice(hp * PEER_NKEYS, (hp + 1) * PEER_NKEYS)
        qs = q[:, hp * LANES:(hp + 1) * LANES].astype(BF16)
        sc_ref[rows, :] = lax.dot_general(keys_ref[hp], qs, NT_DIMS, preferred_element_type=F32)


def _out(merged, x2d, ada3, w_out, gain, w_q, keys, seq, tm):
    tokens = x2d.shape[0]
    per_batch = seq // tm
    nsc = 2 * PEER_HEADS * PEER_NKEYS
    return pl.pallas_call(
        _out_kernel,
        out_shape=(jax.ShapeDtypeStruct((tokens, D_MODEL), F32),
                   jax.ShapeDtypeStruct((tokens, D_MODEL), BF16),
                   jax.ShapeDtypeStruct((nsc, tokens), F32)),
        grid=(tokens // tm,),
        in_specs=[pl.BlockSpec((tm, D_MODEL), lambda i: (i, 0)),
                  pl.BlockSpec((tm, D_MODEL), lambda i: (i, 0)),
                  pl.BlockSpec((None, 6, D_MODEL), lambda i: (i // per_batch, 0, 0)),
                  pl.BlockSpec((D_MODEL, D_MODEL), lambda i: (0, 0)),
                  pl.BlockSpec((1, D_MODEL), lambda i: (0, 0)),
                  pl.BlockSpec(w_q.shape, lambda i: (0, 0)),
                  pl.BlockSpec(keys.shape, lambda i: (0, 0, 0))],
        out_specs=(pl.BlockSpec((tm, D_MODEL), lambda i: (i, 0)),
                   pl.BlockSpec((tm, D_MODEL), lambda i: (i, 0)),
                   pl.BlockSpec((nsc, tm), lambda i: (0, i))),
        compiler_params=_params(("arbitrary",)),
        name="out",
    )(merged, x2d, ada3, w_out, gain, w_q, keys)


SUBLANES = 8


def _sort_pairs(n):
    pairs = []
    p = 1
    while p < n:
        k = p
        while k >= 1:
            for j in range(k % p, n - k, 2 * k):
                for i in range(min(k, n - j - k)):
                    if (i + j) // (2 * p) == (i + j + k) // (2 * p):
                        pairs.append((i + j, i + j + k))
            k //= 2
        p *= 2
    return pairs


def _compare_exchange(vals, i, j):
    a, b = vals[i], vals[j]
    if b is None:
        return
    if a is None:
        vals[i], vals[j] = b, None
        return
    vals[i], vals[j] = jnp.maximum(a, b), jnp.minimum(a, b)


def _top16_sorted(blocks):
    k = PEER_TOPK
    vals = list(blocks) + [None] * (k - len(blocks))
    for i, j in _sort_pairs(k):
        _compare_exchange(vals, i, j)
    shift = SUBLANES // 2
    while shift >= 1:
        other = [None if v is None else pltpu.roll(v, shift, axis=0) for v in vals]
        merged = []
        for i in range(k):
            a, b = vals[i], other[k - 1 - i]
            merged.append(b if a is None else a if b is None else jnp.maximum(a, b))
        vals = merged
        d = k // 2
        while d >= 1:
            for i in range(k):
                if i & d == 0:
                    _compare_exchange(vals, i, i + d)
            d //= 2
        shift //= 2
    return vals


def _route_kernel(sc_ref, r1_ref, b_ref, n_ref, a_ref):
    tl = sc_ref.shape[1]
    k = PEER_TOPK
    nblk = PEER_NKEYS // SUBLANES
    sub = lax.broadcasted_iota(jnp.int32, (SUBLANES, tl), 0)

    def rows_from(vals):
        out = vals[0]
        for r in range(1, SUBLANES):
            out = jnp.where(sub == r, vals[r], out)
        return out

    for h in range(PEER_HEADS):
        base0, base1 = (2 * h) * PEER_NKEYS, (2 * h + 1) * PEER_NKEYS
        s0 = [sc_ref[base0 + q * SUBLANES:base0 + (q + 1) * SUBLANES, :] for q in range(nblk)]
        s1 = [sc_ref[base1 + q * SUBLANES:base1 + (q + 1) * SUBLANES, :] for q in range(nblk)]
        v0 = _top16_sorted(s0)
        v1 = _top16_sorted(s1)
        v1_lo, v1_hi, v0_hi = rows_from(v1[0:8]), rows_from(v1[8:16]), rows_from(v0[8:16])
        cands = [v0[0] + v1_lo, v0[0] + v1_hi]
        cands += [v0[a] + v1_lo for a in range(1, 8)]
        cands += [v0_hi + v1[0]]
        tau = _top16_sorted(cands)[k - 1]
        top = v0[0] + v1[0]
        sel = [cc >= tau for cc in cands]
        z = functools.reduce(jnp.add, [jnp.where(ss, jnp.exp(cc - top), 0.0) for ss, cc in zip(sel, cands)])
        z = jnp.sum(z, axis=0, keepdims=True)
        cnts = [jnp.where(ss, 1.0, 0.0) for ss in sel]
        n_rows = [jnp.sum(cnts[0] + cnts[1], axis=0, keepdims=True)]
        n_rows += [jnp.sum(cnts[a + 1], axis=0, keepdims=True) for a in range(1, 8)]
        n_rows += [cnts[9][a:a + 1, :] for a in range(8)]
        inv_z = 1.0 / z
        for q in range(0, nblk, 2):
            rank, count = [], []
            for qq in (q, q + 1):
                r = jnp.zeros((SUBLANES, tl), F32)
                for kk in range(k):
                    r = r + jnp.where(v1[kk] > s1[qq], 1.0, 0.0)
                rank.append(r)
                c = jnp.zeros((SUBLANES, tl), F32)
                for a in range(k):
                    c = jnp.where(s0[qq] == v0[a], n_rows[a], c)
                count.append(c)
            rows = slice(q * SUBLANES, (q + 2) * SUBLANES)
            r1_ref[h, rows, :] = jnp.concatenate(rank, axis=0).astype(r1_ref.dtype)
            gate1 = jnp.exp(jnp.concatenate([s1[q], s1[q + 1]], axis=0) - v1[0][0:1]) * inv_z
            b_ref[h, rows, :] = gate1.astype(b_ref.dtype)
            n_ref[h, rows, :] = jnp.concatenate(count, axis=0)
            a_ref[h, rows, :] = jnp.exp(jnp.concatenate([s0[q], s0[q + 1]], axis=0) - v0[0][0:1])


def _route(scores_t, tl):
    tokens = scores_t.shape[1]
    shp = (PEER_HEADS, PEER_NKEYS, tokens)
    blk = pl.BlockSpec((PEER_HEADS, PEER_NKEYS, tl), lambda i: (0, 0, i))
    return pl.pallas_call(
        _route_kernel,
        out_shape=(jax.ShapeDtypeStruct(shp, BF16), jax.ShapeDtypeStruct(shp, BF16),
                   jax.ShapeDtypeStruct(shp, F32), jax.ShapeDtypeStruct(shp, F32)),
        grid=(tokens // tl,),
        in_specs=[pl.BlockSpec((scores_t.shape[0], tl), lambda i: (0, i))],
        out_specs=(blk, blk, blk, blk),
        compiler_params=_params(("arbitrary",)),
        name="route",
    )(scores_t)


def _peer_kernel(h2_ref, u_ref, vt_ref, r1_ref, b_ref, n_ref, a_ref, x1_ref, ada_ref, fg_ref, o_ref,
                 hid_ref, p_ref, acc_ref, *, tb, eb):
    e = pl.program_id(1)
    sub = 16

    @pl.when(e == 0)
    def _():
        acc_ref[...] = jnp.zeros_like(acc_ref)

    hid_ref[...] = lax.dot_general(u_ref[...], h2_ref[...], NT_DIMS, preferred_element_type=F32)

    for ii in range(eb // PEER_NKEYS):
        for lc in range(tb // LANES):
            lanes = slice(lc * LANES, (lc + 1) * LANES)
            n_h = [jnp.broadcast_to(n_ref[h, ii:ii + 1, lanes], (sub, LANES)).astype(BF16)
                   for h in range(PEER_HEADS)]
            a_h = [jnp.broadcast_to(a_ref[h, ii:ii + 1, lanes], (sub, LANES)).astype(BF16)
                   for h in range(PEER_HEADS)]
            for g in range(PEER_NKEYS // sub):
                jr = slice(g * sub, (g + 1) * sub)
                rr = slice(ii * PEER_NKEYS + g * sub, ii * PEER_NKEYS + (g + 1) * sub)
                gate = None
                for h in range(PEER_HEADS):
                    term = jnp.where(r1_ref[h, jr, lanes] < n_h[h], b_ref[h, jr, lanes],
                                     jnp.zeros((), BF16)) * a_h[h]
                    gate = term if gate is None else gate + term
                hid = hid_ref[rr, lanes]
                act = 0.5 * hid * (1.0 + lax.erf(hid * INV_SQRT2))
                p_ref[rr, lanes] = gate * act.astype(BF16)
    acc_ref[...] += jnp.dot(vt_ref[...], p_ref[...], preferred_element_type=F32)

    @pl.when(e == pl.num_programs(1) - 1)
    def _():
        x2 = x1_ref[...] + (1.0 + ada_ref[5:6, :]) * acc_ref[...].T
        o_ref[...] = _rms(x2) * fg_ref[...]


def _peer(h2, u, vt, r1t, bt, nt, at, x1, ada3, final_gain, seq, tb, eb):
    tokens = h2.shape[0]
    per_batch = seq // tb
    keys_per_tile = eb // PEER_NKEYS
    rt_blk = pl.BlockSpec((PEER_HEADS, PEER_NKEYS, tb), lambda i, e: (0, 0, i))
    key_blk = pl.BlockSpec((PEER_HEADS, keys_per_tile, tb), lambda i, e: (0, e, i))
    return pl.pallas_call(
        functools.partial(_peer_kernel, tb=tb, eb=eb),
        out_shape=jax.ShapeDtypeStruct((tokens, D_MODEL), F32),
        grid=(tokens // tb, PEER_EXPERTS // eb),
        in_specs=[pl.BlockSpec((tb, D_MODEL), lambda i, e: (i, 0)),
                  pl.BlockSpec((eb, D_MODEL), lambda i, e: (e, 0)),
                  pl.BlockSpec((D_MODEL, eb), lambda i, e: (0, e)),
                  rt_blk, rt_blk, key_blk, key_blk,
                  pl.BlockSpec((tb, D_MODEL), lambda i, e: (i, 0)),
                  pl.BlockSpec((None, 6, D_MODEL), lambda i, e: (i // per_batch, 0, 0)),
                  pl.BlockSpec((1, D_MODEL), lambda i, e: (0, 0))],
        out_specs=pl.BlockSpec((tb, D_MODEL), lambda i, e: (i, 0)),
        scratch_shapes=[pltpu.VMEM((eb, tb), F32),
                        pltpu.VMEM((eb, tb), BF16),
                        pltpu.VMEM((D_MODEL, tb), F32)],
        compiler_params=_params(("arbitrary", "arbitrary")),
        name="peer",
    )(h2, u, vt, r1t, bt, nt, at, x1, ada3, final_gain)


def kernel(x, c, w_ada, b_ada, norm1, w_in, pool_w, pool_scale, lb_logits, hg_norm, w_out,
           norm2, peer_wq, peer_keys, peer_u, peer_v, final_norm):
    batch, seq, d = x.shape
    assert d == D_MODEL and w_ada.shape[0] == 1 and lb_logits.shape[0] == 2
    assert seq % HG_CHUNK == 0 and seq >= POOL_HISTORY
    t = _tiles(batch, seq)
    x2d = x.reshape(batch * seq, d)
    row = lambda v: v.reshape(1, -1).astype(F32)

    ada3 = _ada(c, w_ada[0], b_ada[0].reshape(1, -1), t["ada_tn"]).reshape(batch, 6, d)
    proj = _proj(x2d, ada3, row(norm1[0]), w_in[0].astype(BF16), seq, t["proj_tm"], t["proj_tn"])
    merged = _mixer(proj, lb_logits, row(hg_norm[0]), pool_w[0].astype(BF16), row(pool_scale[0]),
                    jnp.asarray(_decay_sum_matrix(), BF16), batch, seq, t["mix_ts"])
    keys = peer_keys[0].reshape(2 * PEER_HEADS, PEER_NKEYS, -1).astype(BF16)
    x1, h2, scores_t = _out(merged, x2d, ada3, w_out[0].astype(BF16), row(norm2[0]),
                            peer_wq[0].astype(BF16), keys, seq, t["out_tm"])
    r1t, bt, nt, at = _route(scores_t, t["route_tl"])
    y = _peer(h2, peer_u[0].astype(BF16), peer_v[0].T.astype(BF16), r1t, bt, nt, at, x1, ada3,
              row(final_norm), seq, t["peer_tb"], t["peer_eb"])
    return y.reshape(batch, seq, d)
```

```python
import functools

import numpy as np
import jax
import jax.numpy as jnp
from jax import lax
from jax.experimental import pallas as pl
from jax.experimental.pallas import tpu as pltpu

F32 = jnp.float32
BF16 = jnp.bfloat16
EPS = 1e-6

D_MODEL = 2048
LANES = 128
POOL_WINDOWS = (2, 4, 8, 16)
POOL_WIDTH = D_MODEL // 2
POOL_GROUP_DIM = POOL_WIDTH // len(POOL_WINDOWS)
POOL_OUT_DIM = D_MODEL // len(POOL_WINDOWS)
POOL_HISTORY = 16
HG_HEADS = 16
HG_DIM = 128
HG_CHUNK = 64
HG_LEVELS = (32, 16, 8, 4, 2, 1)
PEER_HEADS = 8
PEER_NKEYS = 128
PEER_TOPK = 16
PEER_EXPERTS = PEER_NKEYS * PEER_NKEYS
INV_SQRT2 = 0.7071067811865476

OFF_POOL = 0
OFF_Q = OFF_POOL + POOL_WIDTH
OFF_F = OFF_Q + D_MODEL
OFF_I = OFF_F + D_MODEL
OFF_G = OFF_I + D_MODEL
OFF_GA = OFF_G + D_MODEL
OFF_GB = OFF_GA + D_MODEL
IN_WIDTH = OFF_GB + D_MODEL

VMEM_LIMIT_BYTES = 58 * 1024 * 1024

NT_DIMS = (((1,), (1,)), ((), ()))
TN_DIMS = (((0,), (0,)), ((), ()))


def _tiles(batch, seq):
    tokens = batch * seq
    return dict(
        ada_tn=1024,
        proj_tm=min(1024, seq), proj_tn=1024,
        mix_ts=min(256, seq),
        out_tm=min(256, seq),
        route_tl=LANES,
        peer_tb=min(512, seq), peer_eb=1024,
        tokens=tokens,
    )


def _params(semantics):
    return pltpu.CompilerParams(dimension_semantics=semantics, vmem_limit_bytes=VMEM_LIMIT_BYTES)


def _rms(x):
    return x * lax.rsqrt(jnp.mean(x * x, axis=-1, keepdims=True) + EPS)


def _ada_kernel(c_ref, w_ref, b_ref, o_ref):
    c = c_ref[...]
    cs = c * jax.nn.sigmoid(c)
    o_ref[...] = jnp.dot(cs.astype(BF16), w_ref[...].astype(BF16), preferred_element_type=F32) + b_ref[...]


def _ada(c, w, b, tn):
    bsz, n = c.shape[0], w.shape[1]
    return pl.pallas_call(
        _ada_kernel,
        out_shape=jax.ShapeDtypeStruct((bsz, n), F32),
        grid=(n // tn,),
        in_specs=[pl.BlockSpec((bsz, D_MODEL), lambda j: (0, 0)),
                  pl.BlockSpec((D_MODEL, tn), lambda j: (0, j)),
                  pl.BlockSpec((1, tn), lambda j: (0, j))],
        out_specs=pl.BlockSpec((bsz, tn), lambda j: (0, j)),
        compiler_params=_params(("arbitrary",)),
        name="ada",
    )(c, w, b)


def _proj_kernel(x_ref, ada_ref, g_ref, w_ref, o_ref, h_ref):
    @pl.when(pl.program_id(1) == 0)
    def _():
        y = _rms(x_ref[...]) * g_ref[...]
        h_ref[...] = (y * (1.0 + ada_ref[1:2, :]) + ada_ref[0:1, :]).astype(BF16)

    o_ref[...] = jnp.dot(h_ref[...], w_ref[...], preferred_element_type=F32).astype(o_ref.dtype)


def _proj(x2d, ada3, gain, w, seq, tm, tn):
    tokens, n = x2d.shape[0], w.shape[1]
    per_batch = seq // tm
    return pl.pallas_call(
        _proj_kernel,
        out_shape=jax.ShapeDtypeStruct((tokens, n), BF16),
        grid=(tokens // tm, n // tn),
        in_specs=[pl.BlockSpec((tm, D_MODEL), lambda i, j: (i, 0)),
                  pl.BlockSpec((None, 6, D_MODEL), lambda i, j: (i // per_batch, 0, 0)),
                  pl.BlockSpec((1, D_MODEL), lambda i, j: (0, 0)),
                  pl.BlockSpec((D_MODEL, tn), lambda i, j: (0, j))],
        out_specs=pl.BlockSpec((tm, tn), lambda i, j: (i, j)),
        scratch_shapes=[pltpu.VMEM((tm, D_MODEL), BF16)],
        compiler_params=_params(("arbitrary", "arbitrary")),
        name="proj",
    )(x2d, ada3, gain, w)


def _decay_sum_matrix():
    c = HG_CHUNK
    groups = []
    for m in HG_LEVELS:
        g = np.zeros((c, c), np.float32)
        for t in range(c):
            ref = (t // (2 * m)) * 2 * m + m - 1
            if (t // m) % 2 == 1:
                g[t, ref + 1:t + 1] = 1.0
            else:
                g[t, t + 1:ref + 1] = 1.0
        groups.append(g)
    groups.append(np.tril(np.ones((c, c), np.float32)))
    groups.append(np.triu(np.ones((c, c), np.float32), 1))
    n = np.concatenate(groups, axis=0)
    return np.concatenate([n, n, n], axis=1)


def _forget_gate(z, lb):
    t = jnp.exp(-jnp.abs(z))
    r = 1.0 / (1.0 + t)
    pos = z >= 0.0
    sig_pos = jnp.where(pos, r, t * r)
    sig_neg = jnp.where(pos, t * r, r)
    return jnp.log(lb + (1.0 - lb) * sig_pos), (1.0 - lb) * sig_neg


def _mixer_kernel(proj_ref, lbl_ref, hgn_ref, pw_ref, ps_ref, n_ref, o_ref,
                  st_ref, pbuf_ref, ya_ref, lf3_ref, kk_ref, e_ref, attn_ref, oi_ref, *, ts):
    sblk = pl.program_id(1)
    hist = POOL_HISTORY

    @pl.when(sblk == 0)
    def _():
        st_ref[...] = jnp.zeros_like(st_ref)
        pbuf_ref[0:hist, :] = jnp.zeros((hist, POOL_WIDTH), F32)

    pbuf_ref[hist:hist + ts, :] = proj_ref[:, OFF_POOL:OFF_POOL + POOL_WIDTH].astype(F32)
    pos1 = (sblk * ts + lax.broadcasted_iota(jnp.int32, (ts, 1), 0) + 1).astype(F32)
    for g, w in enumerate(POOL_WINDOWS):
        cols = slice(g * POOL_GROUP_DIM, (g + 1) * POOL_GROUP_DIM)
        cur = pbuf_ref[hist:hist + ts, cols]
        acc = cur
        for back in range(1, w):
            acc = acc + pbuf_ref[hist - back:hist - back + ts, cols]
        pooled = acc / jnp.minimum(pos1, float(w)) - cur
        oc = slice(g * POOL_OUT_DIM, (g + 1) * POOL_OUT_DIM)
        ya_ref[:, oc] = jnp.dot(pooled.astype(BF16), pw_ref[g], preferred_element_type=F32) * ps_ref[:, oc]
    pbuf_ref[0:hist, :] = pbuf_ref[ts:ts + hist, :]

    lbl = lbl_ref[...]
    lbe = jnp.exp(lbl - jnp.max(lbl, axis=0, keepdims=True))
    lb_all = lbe[0:1, :] / jnp.sum(lbe, axis=0, keepdims=True)

    c = HG_CHUNK
    nmat = n_ref[...]
    row = lax.broadcasted_iota(jnp.int32, (c, HG_DIM), 0)
    row_c = lax.broadcasted_iota(jnp.int32, (c, c), 0)
    col_c = lax.broadcasted_iota(jnp.int32, (c, c), 1)

    def chunk_body(ci, carry):
        rows = pl.ds(pl.multiple_of(ci * c, c), c)

        def seg(off, h):
            return proj_ref[rows, off + h * HG_DIM:off + (h + 1) * HG_DIM].astype(F32)

        for hp in range(0, HG_HEADS, 2):
            for h in (hp, hp + 1):
                hc = slice(h * HG_DIM, (h + 1) * HG_DIM)
                lf, kk = _forget_gate(seg(OFF_F, h), lb_all[:, hc])
                hi = lf.astype(BF16)
                rem = lf - hi.astype(F32)
                mid = rem.astype(BF16)
                lf3_ref[0:c, hc] = hi
                lf3_ref[c:2 * c, hc] = mid
                lf3_ref[2 * c:3 * c, hc] = (rem - mid.astype(F32)).astype(BF16)
                kk_ref[:, hc] = kk
            pc = slice(hp * HG_DIM, (hp + 2) * HG_DIM)
            e_ref[:, pc] = jnp.dot(nmat, lf3_ref[:, pc], preferred_element_type=F32)

        for h in range(HG_HEADS):
            hc = slice(h * HG_DIM, (h + 1) * HG_DIM)
            q, kk, iv = seg(OFF_Q, h), kk_ref[:, hc], seg(OFF_I, h)
            attn = jnp.zeros((c, c), F32)
            for li, m in enumerate(HG_LEVELS):
                right = (row & m) != 0
                xs = jnp.where(right, q, kk) * jnp.exp(e_ref[li * c:(li + 1) * c, hc])
                ql = jnp.where(right, xs, 0.0).astype(BF16)
                kl = jnp.where(right, 0.0, xs).astype(BF16)
                a = lax.dot_general(ql, kl, NT_DIMS, preferred_element_type=F32)
                if 2 * m < c:
                    sh = int(np.log2(2 * m))
                    a = jnp.where((row_c >> sh) == (col_c >> sh), a, 0.0)
                attn = attn + a
            attn_ref[h] = attn.astype(BF16)
            dc = jnp.exp(e_ref[6 * c:7 * c, hc])
            dr = jnp.exp(e_ref[7 * c:8 * c, hc])
            st = st_ref[h]
            carried = lax.dot_general((q * dc).astype(BF16), st.astype(BF16), NT_DIMS,
                                      preferred_element_type=F32)
            oi_ref[:, hc] = carried + jnp.sum(q * kk, axis=-1, keepdims=True) * iv
            upd = lax.dot_general(iv.astype(BF16), (kk * dr).astype(BF16), TN_DIMS,
                                  preferred_element_type=F32)
            st_ref[h] = st * dc[c - 1:c, :] + upd

        for h in range(HG_HEADS):
            hc = slice(h * HG_DIM, (h + 1) * HG_DIM)
            o = jnp.dot(attn_ref[h], seg(OFF_I, h).astype(BF16), preferred_element_type=F32) + oi_ref[:, hc]
            g = seg(OFF_G, h)
            yb = _rms(o) * hgn_ref[:, hc] * (g * jax.nn.sigmoid(g))
            merged = (jax.nn.sigmoid(seg(OFF_GA, h)) * ya_ref[rows, hc]
                      + jax.nn.sigmoid(seg(OFF_GB, h)) * yb)
            o_ref[rows, hc] = merged.astype(o_ref.dtype)
        return carry

    lax.fori_loop(0, ts // c, chunk_body, 0)


def _mixer(proj, lb_logits, hg_norm, pool_w, pool_scale, nmat, batch, seq, ts):
    tokens = proj.shape[0]
    per_batch = seq // ts
    return pl.pallas_call(
        functools.partial(_mixer_kernel, ts=ts),
        out_shape=jax.ShapeDtypeStruct((tokens, D_MODEL), BF16),
        grid=(batch, per_batch),
        in_specs=[pl.BlockSpec((ts, IN_WIDTH), lambda b, s: (b * per_batch + s, 0)),
                  pl.BlockSpec(lb_logits.shape, lambda b, s: (0, 0)),
                  pl.BlockSpec((1, D_MODEL), lambda b, s: (0, 0)),
                  pl.BlockSpec(pool_w.shape, lambda b, s: (0, 0, 0)),
                  pl.BlockSpec((1, D_MODEL), lambda b, s: (0, 0)),
                  pl.BlockSpec(nmat.shape, lambda b, s: (0, 0))],
        out_specs=pl.BlockSpec((ts, D_MODEL), lambda b, s: (b * per_batch + s, 0)),
        scratch_shapes=[pltpu.VMEM((HG_HEADS, HG_DIM, HG_DIM), F32),
                        pltpu.VMEM((POOL_HISTORY + ts, POOL_WIDTH), F32),
                        pltpu.VMEM((ts, D_MODEL), F32),
                        pltpu.VMEM((3 * HG_CHUNK, D_MODEL), BF16),
                        pltpu.VMEM((HG_CHUNK, D_MODEL), F32),
                        pltpu.VMEM((nmat.shape[0], D_MODEL), F32),
                        pltpu.VMEM((HG_HEADS, HG_CHUNK, HG_CHUNK), BF16),
                        pltpu.VMEM((HG_CHUNK, D_MODEL), F32)],
        compiler_params=_params(("arbitrary", "arbitrary")),
        name="mixer",
    )(proj, lb_logits, hg_norm, pool_w, pool_scale, nmat)


def _out_kernel(m_ref, x_ref, ada_ref, wo_ref, g_ref, wq_ref, keys_ref, x1_ref, h2_ref, sc_ref):
    y = jnp.dot(m_ref[...], wo_ref[...], preferred_element_type=F32)
    x1 = x_ref[...] + (1.0 + ada_ref[2:3, :]) * y
    x1_ref[...] = x1
    h2 = ((_rms(x1) * g_ref[...]) * (1.0 + ada_ref[4:5, :]) + ada_ref[3:4, :]).astype(BF16)
    h2_ref[...] = h2
    q = jnp.dot(h2, wq_ref[...], preferred_element_type=F32)
    for hp in range(2 * PEER_HEADS):
        rows = slice(hp * PEER_NKEYS, (hp + 1) * PEER_NKEYS)
        qs = q[:, hp * LANES:(hp + 1) * LANES].astype(BF16)
        sc_ref[rows, :] = lax.dot_general(keys_ref[hp], qs, NT_DIMS, preferred_element_type=F32)


def _out(merged, x2d, ada3, w_out, gain, w_q, keys, seq, tm):
    tokens = x2d.shape[0]
    per_batch = seq // tm
    nsc = 2 * PEER_HEADS * PEER_NKEYS
    return pl.pallas_call(
        _out_kernel,
        out_shape=(jax.ShapeDtypeStruct((tokens, D_MODEL), F32),
                   jax.ShapeDtypeStruct((tokens, D_MODEL), BF16),
                   jax.ShapeDtypeStruct((nsc, tokens), F32)),
        grid=(tokens // tm,),
        in_specs=[pl.BlockSpec((tm, D_MODEL), lambda i: (i, 0)),
                  pl.BlockSpec((tm, D_MODEL), lambda i: (i, 0)),
                  pl.BlockSpec((None, 6, D_MODEL), lambda i: (i // per_batch, 0, 0)),
                  pl.BlockSpec((D_MODEL, D_MODEL), lambda i: (0, 0)),
                  pl.BlockSpec((1, D_MODEL), lambda i: (0, 0)),
                  pl.BlockSpec(w_q.shape, lambda i: (0, 0)),
                  pl.BlockSpec(keys.shape, lambda i: (0, 0, 0))],
        out_specs=(pl.BlockSpec((tm, D_MODEL), lambda i: (i, 0)),
                   pl.BlockSpec((tm, D_MODEL), lambda i: (i, 0)),
                   pl.BlockSpec((nsc, tm), lambda i: (0, i))),
        compiler_params=_params(("arbitrary",)),
        name="out",
    )(merged, x2d, ada3, w_out, gain, w_q, keys)


SUBLANES = 8


def _sort_pairs(n):
    pairs = []
    p = 1
    while p < n:
        k = p
        while k >= 1:
            for j in range(k % p, n - k, 2 * k):
                for i in range(min(k, n - j - k)):
                    if (i + j) // (2 * p) == (i + j + k) // (2 * p):
                        pairs.append((i + j, i + j + k))
            k //= 2
        p *= 2
    return pairs


def _compare_exchange(vals, i, j):
    a, b = vals[i], vals[j]
    if b is None:
        return
    if a is None:
        vals[i], vals[j] = b, None
        return
    vals[i], vals[j] = jnp.maximum(a, b), jnp.minimum(a, b)


def _top16_sorted(blocks):
    k = PEER_TOPK
    vals = list(blocks) + [None] * (k - len(blocks))
    for i, j in _sort_pairs(k):
        _compare_exchange(vals, i, j)
    shift = SUBLANES // 2
    while shift >= 1:
        other = [None if v is None else pltpu.roll(v, shift, axis=0) for v in vals]
        merged = []
        for i in range(k):
            a, b = vals[i], other[k - 1 - i]
            merged.append(b if a is None else a if b is None else jnp.maximum(a, b))
        vals = merged
        d = k // 2
        while d >= 1:
            for i in range(k):
                if i & d == 0:
                    _compare_exchange(vals, i, i + d)
            d //= 2
        shift //= 2
    return vals


def _route_kernel(sc_ref, r1_ref, b_ref, n_ref, a_ref):
    tl = sc_ref.shape[1]
    k = PEER_TOPK
    nblk = PEER_NKEYS // SUBLANES
    sub = lax.broadcasted_iota(jnp.int32, (SUBLANES, tl), 0)

    def rows_from(vals):
        out = vals[0]
        for r in range(1, SUBLANES):
            out = jnp.where(sub == r, vals[r], out)
        return out

    for h in range(PEER_HEADS):
        base0, base1 = (2 * h) * PEER_NKEYS, (2 * h + 1) * PEER_NKEYS
        s0 = [sc_ref[base0 + q * SUBLANES:base0 + (q + 1) * SUBLANES, :] for q in range(nblk)]
        s1 = [sc_ref[base1 + q * SUBLANES:base1 + (q + 1) * SUBLANES, :] for q in range(nblk)]
        v0 = _top16_sorted(s0)
        v1 = _top16_sorted(s1)
        v1_lo, v1_hi, v0_hi = rows_from(v1[0:8]), rows_from(v1[8:16]), rows_from(v0[8:16])
        cands = [v0[0] + v1_lo, v0[0] + v1_hi]
        cands += [v0[a] + v1_lo for a in range(1, 8)]
        cands += [v0_hi + v1[0]]
        tau = _top16_sorted(cands)[k - 1]
        top = v0[0] + v1[0]
        sel = [cc >= tau for cc in cands]
        z = functools.reduce(jnp.add, [jnp.where(ss, jnp.exp(cc - top), 0.0) for ss, cc in zip(sel, cands)])
        z = jnp.sum(z, axis=0, keepdims=True)
        cnts = [jnp.where(ss, 1.0, 0.0) for ss in sel]
        n_rows = [jnp.sum(cnts[0] + cnts[1], axis=0, keepdims=True)]
        n_rows += [jnp.sum(cnts[a + 1], axis=0, keepdims=True) for a in range(1, 8)]
        n_rows += [cnts[9][a:a + 1, :] for a in range(8)]
        inv_z = 1.0 / z
        for q in range(0, nblk, 2):
            rank, count = [], []
            for qq in (q, q + 1):
                r = jnp.zeros((SUBLANES, tl), F32)
                for kk in range(k):
                    r = r + jnp.where(v1[kk] > s1[qq], 1.0, 0.0)
                rank.append(r)
                c = jnp.zeros((SUBLANES, tl), F32)
                for a in range(k):
                    c = jnp.where(s0[qq] == v0[a], n_rows[a], c)
                count.append(c)
            rows = slice(q * SUBLANES, (q + 2) * SUBLANES)
            r1_ref[h, rows, :] = jnp.concatenate(rank, axis=0).astype(r1_ref.dtype)
            gate1 = jnp.exp(jnp.concatenate([s1[q], s1[q + 1]], axis=0) - v1[0][0:1]) * inv_z
            b_ref[h, rows, :] = gate1.astype(b_ref.dtype)
            n_ref[h, rows, :] = jnp.concatenate(count, axis=0)
            a_ref[h, rows, :] = jnp.exp(jnp.concatenate([s0[q], s0[q + 1]], axis=0) - v0[0][0:1])


def _route(scores_t, tl):
    tokens = scores_t.shape[1]
    shp = (PEER_HEADS, PEER_NKEYS, tokens)
    blk = pl.BlockSpec((PEER_HEADS, PEER_NKEYS, tl), lambda i: (0, 0, i))
    return pl.pallas_call(
        _route_kernel,
        out_shape=(jax.ShapeDtypeStruct(shp, BF16), jax.ShapeDtypeStruct(shp, BF16),
                   jax.ShapeDtypeStruct(shp, F32), jax.ShapeDtypeStruct(shp, F32)),
        grid=(tokens // tl,),
        in_specs=[pl.BlockSpec((scores_t.shape[0], tl), lambda i: (0, i))],
        out_specs=(blk, blk, blk, blk),
        compiler_params=_params(("arbitrary",)),
        name="route",
    )(scores_t)


def _peer_kernel(h2_ref, u_ref, vt_ref, r1_ref, b_ref, n_ref, a_ref, x1_ref, ada_ref, fg_ref, o_ref,
                 hid_ref, p_ref, acc_ref, *, tb, eb):
    e = pl.program_id(1)
    sub = 16

    @pl.when(e == 0)
    def _():
        acc_ref[...] = jnp.zeros_like(acc_ref)

    hid_ref[...] = lax.dot_general(u_ref[...], h2_ref[...], NT_DIMS, preferred_element_type=F32)

    for ii in range(eb // PEER_NKEYS):
        for lc in range(tb // LANES):
            lanes = slice(lc * LANES, (lc + 1) * LANES)
            n_h = [jnp.broadcast_to(n_ref[h, ii:ii + 1, lanes], (sub, LANES)).astype(BF16)
                   for h in range(PEER_HEADS)]
            a_h = [jnp.broadcast_to(a_ref[h, ii:ii + 1, lanes], (sub, LANES)).astype(BF16)
                   for h in range(PEER_HEADS)]
            for g in range(PEER_NKEYS // sub):
                jr = slice(g * sub, (g + 1) * sub)
                rr = slice(ii * PEER_NKEYS + g * sub, ii * PEER_NKEYS + (g + 1) * sub)
                gate = None
                for h in range(PEER_HEADS):
                    term = jnp.where(r1_ref[h, jr, lanes] < n_h[h], b_ref[h, jr, lanes],
                                     jnp.zeros((), BF16)) * a_h[h]
                    gate = term if gate is None else gate + term
                hid = hid_ref[rr, lanes]
                act = 0.5 * hid * (1.0 + lax.erf(hid * INV_SQRT2))
                p_ref[rr, lanes] = gate * act.astype(BF16)
    acc_ref[...] += jnp.dot(vt_ref[...], p_ref[...], preferred_element_type=F32)

    @pl.when(e == pl.num_programs(1) - 1)
    def _():
        x2 = x1_ref[...] + (1.0 + ada_ref[5:6, :]) * acc_ref[...].T
        o_ref[...] = _rms(x2) * fg_ref[...]


def _peer(h2, u, vt, r1t, bt, nt, at, x1, ada3, final_gain, seq, tb, eb):
    tokens = h2.shape[0]
    per_batch = seq // tb
    keys_per_tile = eb // PEER_NKEYS
    rt_blk = pl.BlockSpec((PEER_HEADS, PEER_NKEYS, tb), lambda i, e: (0, 0, i))
    key_blk = pl.BlockSpec((PEER_HEADS, keys_per_tile, tb), lambda i, e: (0, e, i))
    return pl.pallas_call(
        functools.partial(_peer_kernel, tb=tb, eb=eb),
        out_shape=jax.ShapeDtypeStruct((tokens, D_MODEL), F32),
        grid=(tokens // tb, PEER_EXPERTS // eb),
        in_specs=[pl.BlockSpec((tb, D_MODEL), lambda i, e: (i, 0)),
                  pl.BlockSpec((eb, D_MODEL), lambda i, e: (e, 0)),
                  pl.BlockSpec((D_MODEL, eb), lambda i, e: (0, e)),
                  rt_blk, rt_blk, key_blk, key_blk,
                  pl.BlockSpec((tb, D_MODEL), lambda i, e: (i, 0)),
                  pl.BlockSpec((None, 6, D_MODEL), lambda i, e: (i // per_batch, 0, 0)),
                  pl.BlockSpec((1, D_MODEL), lambda i, e: (0, 0))],
        out_specs=pl.BlockSpec((tb, D_MODEL), lambda i, e: (i, 0)),
        scratch_shapes=[pltpu.VMEM((eb, tb), F32),
                        pltpu.VMEM((eb, tb), BF16),
                        pltpu.VMEM((D_MODEL, tb), F32)],
        compiler_params=_params(("arbitrary", "arbitrary")),
        name="peer",
    )(h2, u, vt, r1t, bt, nt, at, x1, ada3, final_gain)


def kernel(x, c, w_ada, b_ada, norm1, w_in, pool_w, pool_scale, lb_logits, hg_norm, w_out,
           norm2, peer_wq, peer_keys, peer_u, peer_v, final_norm):
    batch, seq, d = x.shape
    assert d == D_MODEL and w_ada.shape[0] == 1 and lb_logits.shape[0] == 2
    assert seq % HG_CHUNK == 0 and seq >= POOL_HISTORY
    t = _tiles(batch, seq)
    x2d = x.reshape(batch * seq, d)
    row = lambda v: v.reshape(1, -1).astype(F32)

    ada3 = _ada(c, w_ada[0], b_ada[0].reshape(1, -1), t["ada_tn"]).reshape(batch, 6, d)
    proj = _proj(x2d, ada3, row(norm1[0]), w_in[0].astype(BF16), seq, t["proj_tm"], t["proj_tn"])
    merged = _mixer(proj, lb_logits, row(hg_norm[0]), pool_w[0].astype(BF16), row(pool_scale[0]),
                    jnp.asarray(_decay_sum_matrix(), BF16), batch, seq, t["mix_ts"])
    keys = peer_keys[0].reshape(2 * PEER_HEADS, PEER_NKEYS, -1).astype(BF16)
    x1, h2, scores_t = _out(merged, x2d, ada3, w_out[0].astype(BF16), row(norm2[0]),
                            peer_wq[0].astype(BF16), keys, seq, t["out_tm"])
    r1t, bt, nt, at = _route(scores_t, t["route_tl"])
    y = _peer(h2, peer_u[0].astype(BF16), peer_v[0].T.astype(BF16), r1t, bt, nt, at, x1, ada3,
              row(final_norm), seq, t["peer_tb"], t["peer_eb"])
    return y.reshape(batch, seq, d)
```

```python
import functools

import numpy as np
import jax
import jax.numpy as jnp
from jax import lax
from jax.experimental import pallas as pl
from jax.experimental.pallas import tpu as pltpu

F32 = jnp.float32
BF16 = jnp.bfloat16
EPS = 1e-6

D_MODEL = 2048
LANES = 128
POOL_WINDOWS = (2, 4, 8, 16)
POOL_WIDTH = D_MODEL // 2
POOL_GROUP_DIM = POOL_WIDTH // len(POOL_WINDOWS)
POOL_OUT_DIM = D_MODEL // len(POOL_WINDOWS)
POOL_HISTORY = 16
HG_HEADS = 16
HG_DIM = 128
HG_CHUNK = 64
HG_LEVELS = (32, 16, 8, 4, 2, 1)
PEER_HEADS = 8
PEER_NKEYS = 128
PEER_TOPK = 16
PEER_EXPERTS = PEER_NKEYS * PEER_NKEYS
INV_SQRT2 = 0.7071067811865476

OFF_POOL = 0
OFF_Q = OFF_POOL + POOL_WIDTH
OFF_F = OFF_Q + D_MODEL
OFF_I = OFF_F + D_MODEL
OFF_G = OFF_I + D_MODEL
OFF_GA = OFF_G + D_MODEL
OFF_GB = OFF_GA + D_MODEL
IN_WIDTH = OFF_GB + D_MODEL

VMEM_LIMIT_BYTES = 58 * 1024 * 1024

NT_DIMS = (((1,), (1,)), ((), ()))
TN_DIMS = (((0,), (0,)), ((), ()))


def _tiles(batch, seq):
    tokens = batch * seq
    return dict(
        ada_tn=1024,
        proj_tm=min(1024, seq), proj_tn=1024,
        mix_ts=min(256, seq),
        out_tm=min(256, seq),
        route_tl=LANES,
        peer_tb=min(512, seq), peer_eb=1024,
        tokens=tokens,
    )


def _params(semantics):
    return pltpu.CompilerParams(dimension_semantics=semantics, vmem_limit_bytes=VMEM_LIMIT_BYTES)


def _rms(x):
    return x * lax.rsqrt(jnp.mean(x * x, axis=-1, keepdims=True) + EPS)


def _ada_kernel(c_ref, w_ref, b_ref, o_ref):
    c = c_ref[...]
    cs = c * jax.nn.sigmoid(c)
    o_ref[...] = jnp.dot(cs.astype(BF16), w_ref[...].astype(BF16), preferred_element_type=F32) + b_ref[...]


def _ada(c, w, b, tn):
    bsz, n = c.shape[0], w.shape[1]
    return pl.pallas_call(
        _ada_kernel,
        out_shape=jax.ShapeDtypeStruct((bsz, n), F32),
        grid=(n // tn,),
        in_specs=[pl.BlockSpec((bsz, D_MODEL), lambda j: (0, 0)),
                  pl.BlockSpec((D_MODEL, tn), lambda j: (0, j)),
                  pl.BlockSpec((1, tn), lambda j: (0, j))],
        out_specs=pl.BlockSpec((bsz, tn), lambda j: (0, j)),
        compiler_params=_params(("arbitrary",)),
        name="ada",
    )(c, w, b)


def _proj_kernel(x_ref, ada_ref, g_ref, w_ref, o_ref, h_ref):
    @pl.when(pl.program_id(1) == 0)
    def _():
        y = _rms(x_ref[...]) * g_ref[...]
        h_ref[...] = (y * (1.0 + ada_ref[1:2, :]) + ada_ref[0:1, :]).astype(BF16)

    o_ref[...] = jnp.dot(h_ref[...], w_ref[...], preferred_element_type=F32).astype(o_ref.dtype)


def _proj(x2d, ada3, gain, w, seq, tm, tn):
    tokens, n = x2d.shape[0], w.shape[1]
    per_batch = seq // tm
    return pl.pallas_call(
        _proj_kernel,
        out_shape=jax.ShapeDtypeStruct((tokens, n), BF16),
        grid=(tokens // tm, n // tn),
        in_specs=[pl.BlockSpec((tm, D_MODEL), lambda i, j: (i, 0)),
                  pl.BlockSpec((None, 6, D_MODEL), lambda i, j: (i // per_batch, 0, 0)),
                  pl.BlockSpec((1, D_MODEL), lambda i, j: (0, 0)),
                  pl.BlockSpec((D_MODEL, tn), lambda i, j: (0, j))],
        out_specs=pl.BlockSpec((tm, tn), lambda i, j: (i, j)),
        scratch_shapes=[pltpu.VMEM((tm, D_MODEL), BF16)],
        compiler_params=_params(("arbitrary", "arbitrary")),
        name="proj",
    )(x2d, ada3, gain, w)


def _decay_sum_matrix():
    c = HG_CHUNK
    groups = []
    for m in HG_LEVELS:
        g = np.zeros((c, c), np.float32)
        for t in range(c):
            ref = (t // (2 * m)) * 2 * m + m - 1
            if (t // m) % 2 == 1:
                g[t, ref + 1:t + 1] = 1.0
            else:
                g[t, t + 1:ref + 1] = 1.0
        groups.append(g)
    groups.append(np.tril(np.ones((c, c), np.float32)))
    groups.append(np.triu(np.ones((c, c), np.float32), 1))
    n = np.concatenate(groups, axis=0)
    return np.concatenate([n, n, n], axis=1)


def _forget_gate(z, lb):
    t = jnp.exp(-jnp.abs(z))
    r = 1.0 / (1.0 + t)
    pos = z >= 0.0
    sig_pos = jnp.where(pos, r, t * r)
    sig_neg = jnp.where(pos, t * r, r)
    return jnp.log(lb + (1.0 - lb) * sig_pos), (1.0 - lb) * sig_neg


def _mixer_kernel(proj_ref, lbl_ref, hgn_ref, pw_ref, ps_ref, n_ref, o_ref,
                  st_ref, pbuf_ref, ya_ref, lf3_ref, kk_ref, e_ref, attn_ref, oi_ref, *, ts):
    sblk = pl.program_id(1)
    hist = POOL_HISTORY

    @pl.when(sblk == 0)
    def _():
        st_ref[...] = jnp.zeros_like(st_ref)
        pbuf_ref[0:hist, :] = jnp.zeros((hist, POOL_WIDTH), F32)

    pbuf_ref[hist:hist + ts, :] = proj_ref[:, OFF_POOL:OFF_POOL + POOL_WIDTH].astype(F32)
    pos1 = (sblk * ts + lax.broadcasted_iota(jnp.int32, (ts, 1), 0) + 1).astype(F32)
    for g, w in enumerate(POOL_WINDOWS):
        cols = slice(g * POOL_GROUP_DIM, (g + 1) * POOL_GROUP_DIM)
        cur = pbuf_ref[hist:hist + ts, cols]
        acc = cur
        for back in range(1, w):
            acc = acc + pbuf_ref[hist - back:hist - back + ts, cols]
        pooled = acc / jnp.minimum(pos1, float(w)) - cur
        oc = slice(g * POOL_OUT_DIM, (g + 1) * POOL_OUT_DIM)
        ya_ref[:, oc] = jnp.dot(pooled.astype(BF16), pw_ref[g], preferred_element_type=F32) * ps_ref[:, oc]
    pbuf_ref[0:hist, :] = pbuf_ref[ts:ts + hist, :]

    lbl = lbl_ref[...]
    lbe = jnp.exp(lbl - jnp.max(lbl, axis=0, keepdims=True))
    lb_all = lbe[0:1, :] / jnp.sum(lbe, axis=0, keepdims=True)

    c = HG_CHUNK
    nmat = n_ref[...]
    row = lax.broadcasted_iota(jnp.int32, (c, HG_DIM), 0)
    row_c = lax.broadcasted_iota(jnp.int32, (c, c), 0)
    col_c = lax.broadcasted_iota(jnp.int32, (c, c), 1)

    def chunk_body(ci, carry):
        rows = pl.ds(pl.multiple_of(ci * c, c), c)

        def seg(off, h):
            return proj_ref[rows, off + h * HG_DIM:off + (h + 1) * HG_DIM].astype(F32)

        for hp in range(0, HG_HEADS, 2):
            for h in (hp, hp + 1):
                hc = slice(h * HG_DIM, (h + 1) * HG_DIM)
                lf, kk = _forget_gate(seg(OFF_F, h), lb_all[:, hc])
                hi = lf.astype(BF16)
                rem = lf - hi.astype(F32)
                mid = rem.astype(BF16)
                lf3_ref[0:c, hc] = hi
                lf3_ref[c:2 * c, hc] = mid
                lf3_ref[2 * c:3 * c, hc] = (rem - mid.astype(F32)).astype(BF16)
                kk_ref[:, hc] = kk
            pc = slice(hp * HG_DIM, (hp + 2) * HG_DIM)
            e_ref[:, pc] = jnp.dot(nmat, lf3_ref[:, pc], preferred_element_type=F32)

        for h in range(HG_HEADS):
            hc = slice(h * HG_DIM, (h + 1) * HG_DIM)
            q, kk, iv = seg(OFF_Q, h), kk_ref[:, hc], seg(OFF_I, h)
            attn = jnp.zeros((c, c), F32)
            for li, m in enumerate(HG_LEVELS):
                right = (row & m) != 0
                xs = jnp.where(right, q, kk) * jnp.exp(e_ref[li * c:(li + 1) * c, hc])
                ql = jnp.where(right, xs, 0.0).astype(BF16)
                kl = jnp.where(right, 0.0, xs).astype(BF16)
                a = lax.dot_general(ql, kl, NT_DIMS, preferred_element_type=F32)
                if 2 * m < c:
                    sh = int(np.log2(2 * m))
                    a = jnp.where((row_c >> sh) == (col_c >> sh), a, 0.0)
                attn = attn + a
            attn_ref[h] = attn.astype(BF16)
            dc = jnp.exp(e_ref[6 * c:7 * c, hc])
            dr = jnp.exp(e_ref[7 * c:8 * c, hc])
            st = st_ref[h]
            carried = lax.dot_general((q * dc).astype(BF16), st.astype(BF16), NT_DIMS,
                                      preferred_element_type=F32)
            oi_ref[:, hc] = carried + jnp.sum(q * kk, axis=-1, keepdims=True) * iv
            upd = lax.dot_general(iv.astype(BF16), (kk * dr).astype(BF16), TN_DIMS,
                                  preferred_element_type=F32)
            st_ref[h] = st * dc[c - 1:c, :] + upd

        for h in range(HG_HEADS):
            hc = slice(h * HG_DIM, (h + 1) * HG_DIM)
            o = jnp.dot(attn_ref[h], seg(OFF_I, h).astype(BF16), preferred_element_type=F32) + oi_ref[:, hc]
            g = seg(OFF_G, h)
            yb = _rms(o) * hgn_ref[:, hc] * (g * jax.nn.sigmoid(g))
            merged = (jax.nn.sigmoid(seg(OFF_GA, h)) * ya_ref[rows, hc]
                      + jax.nn.sigmoid(seg(OFF_GB, h)) * yb)
            o_ref[rows, hc] = merged.astype(o_ref.dtype)
        return carry

    lax.fori_loop(0, ts // c, chunk_body, 0)


def _mixer(proj, lb_logits, hg_norm, pool_w, pool_scale, nmat, batch, seq, ts):
    tokens = proj.shape[0]
    per_batch = seq // ts
    return pl.pallas_call(
        functools.partial(_mixer_kernel, ts=ts),
        out_shape=jax.ShapeDtypeStruct((tokens, D_MODEL), BF16),
        grid=(batch, per_batch),
        in_specs=[pl.BlockSpec((ts, IN_WIDTH), lambda b, s: (b * per_batch + s, 0)),
                  pl.BlockSpec(lb_logits.shape, lambda b, s: (0, 0)),
                  pl.BlockSpec((1, D_MODEL), lambda b, s: (0, 0)),
                  pl.BlockSpec(pool_w.shape, lambda b, s: (0, 0, 0)),
                  pl.BlockSpec((1, D_MODEL), lambda b, s: (0, 0)),
                  pl.BlockSpec(nmat.shape, lambda b, s: (0, 0))],
        out_specs=pl.BlockSpec((ts, D_MODEL), lambda b, s: (b * per_batch + s, 0)),
        scratch_shapes=[pltpu.VMEM((HG_HEADS, HG_DIM, HG_DIM), F32),
                        pltpu.VMEM((POOL_HISTORY + ts, POOL_WIDTH), F32),
                        pltpu.VMEM((ts, D_MODEL), F32),
                        pltpu.VMEM((3 * HG_CHUNK, D_MODEL), BF16),
                        pltpu.VMEM((HG_CHUNK, D_MODEL), F32),
                        pltpu.VMEM((nmat.shape[0], D_MODEL), F32),
                        pltpu.VMEM((HG_HEADS, HG_CHUNK, HG_CHUNK), BF16),
                        pltpu.VMEM((HG_CHUNK, D_MODEL), F32)],
        compiler_params=_params(("arbitrary", "arbitrary")),
        name="mixer",
    )(proj, lb_logits, hg_norm, pool_w, pool_scale, nmat)


def _out_kernel(m_ref, x_ref, ada_ref, wo_ref, g_ref, wq_ref, keys_ref, x1_ref, h2_ref, sc_ref):
    y = jnp.dot(m_ref[...], wo_ref[...], preferred_element_type=F32)
    x1 = x_ref[...] + (1.0 + ada_ref[2:3, :]) * y
    x1_ref[...] = x1
    h2 = ((_rms(x1) * g_ref[...]) * (1.0 + ada_ref[4:5, :]) + ada_ref[3:4, :]).astype(BF16)
    h2_ref[...] = h2
    q = jnp.dot(h2, wq_ref[...], preferred_element_type=F32)
    for hp in range(2 * PEER_HEADS):
        rows = slice(hp * PEER_NKEYS, (hp + 1) * PEER_NKEYS)
        qs = q[:, hp * LANES:(hp + 1) * LANES].astype(BF16)
        sc_ref[rows, :] = lax.dot_general(keys_ref[hp], qs, NT_DIMS, preferred_element_type=F32)


def _out(merged, x2d, ada3, w_out, gain, w_q, keys, seq, tm):
    tokens = x2d.shape[0]
    per_batch = seq // tm
    nsc = 2 * PEER_HEADS * PEER_NKEYS
    return pl.pallas_call(
        _out_kernel,
        out_shape=(jax.ShapeDtypeStruct((tokens, D_MODEL), F32),
                   jax.ShapeDtypeStruct((tokens, D_MODEL), BF16),
                   jax.ShapeDtypeStruct((nsc, tokens), F32)),
        grid=(tokens // tm,),
        in_specs=[pl.BlockSpec((tm, D_MODEL), lambda i: (i, 0)),
                  pl.BlockSpec((tm, D_MODEL), lambda i: (i, 0)),
                  pl.BlockSpec((None, 6, D_MODEL), lambda i: (i // per_batch, 0, 0)),
                  pl.BlockSpec((D_MODEL, D_MODEL), lambda i: (0, 0)),
                  pl.BlockSpec((1, D_MODEL), lambda i: (0, 0)),
                  pl.BlockSpec(w_q.shape, lambda i: (0, 0)),
                  pl.BlockSpec(keys.shape, lambda i: (0, 0, 0))],
        out_specs=(pl.BlockSpec((tm, D_MODEL), lambda i: (i, 0)),
                   pl.BlockSpec((tm, D_MODEL), lambda i: (i, 0)),
                   pl.BlockSpec((nsc, tm), lambda i: (0, i))),
        compiler_params=_params(("arbitrary",)),
        name="out",
    )(merged, x2d, ada3, w_out, gain, w_q, keys)


SUBLANES = 8


def _sort_pairs(n):
    pairs = []
    p = 1
    while p < n:
        k = p
        while k >= 1:
            for j in range(k % p, n - k, 2 * k):
                for i in range(min(k, n - j - k)):
                    if (i + j) // (2 * p) == (i + j + k) // (2 * p):
                        pairs.append((i + j, i + j + k))
            k //= 2
        p *= 2
    return pairs


def _compare_exchange(vals, i, j):
    a, b = vals[i], vals[j]
    if b is None:
        return
    if a is None:
        vals[i], vals[j] = b, None
        return
    vals[i], vals[j] = jnp.maximum(a, b), jnp.minimum(a, b)


def _top16_sorted(blocks):
    k = PEER_TOPK
    vals = list(blocks) + [None] * (k - len(blocks))
    for i, j in _sort_pairs(k):
        _compare_exchange(vals, i, j)
    shift = SUBLANES // 2
    while shift >= 1:
        other = [None if v is None else pltpu.roll(v, shift, axis=0) for v in vals]
        merged = []
        for i in range(k):
            a, b = vals[i], other[k - 1 - i]
            merged.append(b if a is None else a if b is None else jnp.maximum(a, b))
        vals = merged
        d = k // 2
        while d >= 1:
            for i in range(k):
                if i & d == 0:
                    _compare_exchange(vals, i, i + d)
            d //= 2
        shift //= 2
    return vals


def _route_kernel(sc_ref, r1_ref, b_ref, n_ref, a_ref):
    tl = sc_ref.shape[1]
    k = PEER_TOPK
    nblk = PEER_NKEYS // SUBLANES
    sub = lax.broadcasted_iota(jnp.int32, (SUBLANES, tl), 0)

    def rows_from(vals):
        out = vals[0]
        for r in range(1, SUBLANES):
            out = jnp.where(sub == r, vals[r], out)
        return out

    for h in range(PEER_HEADS):
        base0, base1 = (2 * h) * PEER_NKEYS, (2 * h + 1) * PEER_NKEYS
        s0 = [sc_ref[base0 + q * SUBLANES:base0 + (q + 1) * SUBLANES, :] for q in range(nblk)]
        s1 = [sc_ref[base1 + q * SUBLANES:base1 + (q + 1) * SUBLANES, :] for q in range(nblk)]
        v0 = _top16_sorted(s0)
        v1 = _top16_sorted(s1)
        v1_lo, v1_hi, v0_hi = rows_from(v1[0:8]), rows_from(v1[8:16]), rows_from(v0[8:16])
        cands = [v0[0] + v1_lo, v0[0] + v1_hi]
        cands += [v0[a] + v1_lo for a in range(1, 8)]
        cands += [v0_hi + v1[0]]
        tau = _top16_sorted(cands)[k - 1]
        top = v0[0] + v1[0]
        sel = [cc >= tau for cc in cands]
        z = functools.reduce(jnp.add, [jnp.where(ss, jnp.exp(cc - top), 0.0) for ss, cc in zip(sel, cands)])
        z = jnp.sum(z, axis=0, keepdims=True)
        cnts = [jnp.where(ss, 1.0, 0.0) for ss in sel]
        n_rows = [jnp.sum(cnts[0] + cnts[1], axis=0, keepdims=True)]
        n_rows += [jnp.sum(cnts[a + 1], axis=0, keepdims=True) for a in range(1, 8)]
        n_rows += [cnts[9][a:a + 1, :] for a in range(8)]
        inv_z = INV_SQRT2 / z
        for q in range(0, nblk, 2):
            rank, count = [], []
            for qq in (q, q + 1):
                r = jnp.zeros((SUBLANES, tl), F32)
                for kk in range(k):
                    r = r + jnp.where(v1[kk] > s1[qq], 1.0, 0.0)
                rank.append(r)
                c = jnp.zeros((SUBLANES, tl), F32)
                for a in range(k):
                    c = jnp.where(s0[qq] == v0[a], n_rows[a], c)
                count.append(c)
            rows = slice(q * SUBLANES, (q + 2) * SUBLANES)
            r1_ref[h, rows, :] = jnp.concatenate(rank, axis=0).astype(r1_ref.dtype)
            gate1 = jnp.exp(jnp.concatenate([s1[q], s1[q + 1]], axis=0) - v1[0][0:1]) * inv_z
            b_ref[h, rows, :] = gate1.astype(b_ref.dtype)
            n_ref[h, rows, :] = jnp.concatenate(count, axis=0)
            a_ref[h, rows, :] = jnp.exp(jnp.concatenate([s0[q], s0[q + 1]], axis=0) - v0[0][0:1])


def _route(scores_t, tl):
    tokens = scores_t.shape[1]
    shp = (PEER_HEADS, PEER_NKEYS, tokens)
    blk = pl.BlockSpec((PEER_HEADS, PEER_NKEYS, tl), lambda i: (0, 0, i))
    return pl.pallas_call(
        _route_kernel,
        out_shape=(jax.ShapeDtypeStruct(shp, BF16), jax.ShapeDtypeStruct(shp, BF16),
                   jax.ShapeDtypeStruct(shp, F32), jax.ShapeDtypeStruct(shp, F32)),
        grid=(tokens // tl,),
        in_specs=[pl.BlockSpec((scores_t.shape[0], tl), lambda i: (0, i))],
        out_specs=(blk, blk, blk, blk),
        compiler_params=_params(("arbitrary",)),
        name="route",
    )(scores_t)


def _peer_kernel(h2_ref, u_ref, vt_ref, r1_ref, b_ref, n_ref, a_ref, x1_ref, ada_ref, fg_ref, o_ref,
                 hid_ref, p_ref, acc_ref, *, tb, eb):
    e = pl.program_id(1)
    sub = 16

    @pl.when(e == 0)
    def _():
        acc_ref[...] = jnp.zeros_like(acc_ref)

    hid_ref[...] = lax.dot_general(u_ref[...], h2_ref[...], NT_DIMS, preferred_element_type=F32)

    for ii in range(eb // PEER_NKEYS):
        for lc in range(tb // LANES):
            lanes = slice(lc * LANES, (lc + 1) * LANES)
            n_h = [jnp.broadcast_to(n_ref[h, ii:ii + 1, lanes], (sub, LANES)).astype(BF16)
                   for h in range(PEER_HEADS)]
            a_h = [jnp.broadcast_to(a_ref[h, ii:ii + 1, lanes], (sub, LANES)).astype(BF16)
                   for h in range(PEER_HEADS)]
            for g in range(PEER_NKEYS // sub):
                jr = slice(g * sub, (g + 1) * sub)
                rr = slice(ii * PEER_NKEYS + g * sub, ii * PEER_NKEYS + (g + 1) * sub)
                gate = None
                for h in range(PEER_HEADS):
                    term = jnp.where(r1_ref[h, jr, lanes] < n_h[h], b_ref[h, jr, lanes],
                                     jnp.zeros((), BF16)) * a_h[h]
                    gate = term if gate is None else gate + term
                w = hid_ref[rr, lanes].astype(BF16)
                p_ref[rr, lanes] = gate * (w * (1.0 + lax.erf(w)))
    acc_ref[...] += jnp.dot(vt_ref[...], p_ref[...], preferred_element_type=F32)

    @pl.when(e == pl.num_programs(1) - 1)
    def _():
        x2 = x1_ref[...] + (1.0 + ada_ref[5:6, :]) * acc_ref[...].T
        o_ref[...] = _rms(x2) * fg_ref[...]


def _peer(h2, u, vt, r1t, bt, nt, at, x1, ada3, final_gain, seq, tb, eb):
    tokens = h2.shape[0]
    per_batch = seq // tb
    keys_per_tile = eb // PEER_NKEYS
    rt_blk = pl.BlockSpec((PEER_HEADS, PEER_NKEYS, tb), lambda i, e: (0, 0, i))
    key_blk = pl.BlockSpec((PEER_HEADS, keys_per_tile, tb), lambda i, e: (0, e, i))
    return pl.pallas_call(
        functools.partial(_peer_kernel, tb=tb, eb=eb),
        out_shape=jax.ShapeDtypeStruct((tokens, D_MODEL), F32),
        grid=(tokens // tb, PEER_EXPERTS // eb),
        in_specs=[pl.BlockSpec((tb, D_MODEL), lambda i, e: (i, 0)),
                  pl.BlockSpec((eb, D_MODEL), lambda i, e: (e, 0)),
                  pl.BlockSpec((None, D_MODEL, eb), lambda i, e: (e, 0, 0)),
                  rt_blk, rt_blk, key_blk, key_blk,
                  pl.BlockSpec((tb, D_MODEL), lambda i, e: (i, 0)),
                  pl.BlockSpec((None, 6, D_MODEL), lambda i, e: (i // per_batch, 0, 0)),
                  pl.BlockSpec((1, D_MODEL), lambda i, e: (0, 0))],
        out_specs=pl.BlockSpec((tb, D_MODEL), lambda i, e: (i, 0)),
        scratch_shapes=[pltpu.VMEM((eb, tb), F32),
                        pltpu.VMEM((eb, tb), BF16),
                        pltpu.VMEM((D_MODEL, tb), F32)],
        compiler_params=_params(("arbitrary", "arbitrary")),
        name="peer",
    )(h2, u, vt, r1t, bt, nt, at, x1, ada3, final_gain)


def kernel(x, c, w_ada, b_ada, norm1, w_in, pool_w, pool_scale, lb_logits, hg_norm, w_out,
           norm2, peer_wq, peer_keys, peer_u, peer_v, final_norm):
    batch, seq, d = x.shape
    assert d == D_MODEL and w_ada.shape[0] == 1 and lb_logits.shape[0] == 2
    assert seq % HG_CHUNK == 0 and seq >= POOL_HISTORY
    t = _tiles(batch, seq)
    x2d = x.reshape(batch * seq, d)
    row = lambda v: v.reshape(1, -1).astype(F32)

    ada3 = _ada(c, w_ada[0], b_ada[0].reshape(1, -1), t["ada_tn"]).reshape(batch, 6, d)
    proj = _proj(x2d, ada3, row(norm1[0]), w_in[0].astype(BF16), seq, t["proj_tm"], t["proj_tn"])
    merged = _mixer(proj, lb_logits, row(hg_norm[0]), pool_w[0].astype(BF16), row(pool_scale[0]),
                    jnp.asarray(_decay_sum_matrix(), BF16), batch, seq, t["mix_ts"])
    keys = peer_keys[0].reshape(2 * PEER_HEADS, PEER_NKEYS, -1).astype(BF16)
    x1, h2, scores_t = _out(merged, x2d, ada3, w_out[0].astype(BF16), row(norm2[0]),
                            peer_wq[0].astype(BF16), keys, seq, t["out_tm"])
    r1t, bt, nt, at = _route(scores_t, t["route_tl"])
    eb = t["peer_eb"]
    u_scaled = (peer_u[0] * INV_SQRT2).astype(BF16)
    vt = peer_v[0].reshape(PEER_EXPERTS // eb, eb, d).transpose(0, 2, 1).astype(BF16)
    y = _peer(h2, u_scaled, vt, r1t, bt, nt, at, x1, ada3, row(final_norm), seq, t["peer_tb"], eb)
    return y.reshape(batch, seq, d)
```

```python
import functools

import numpy as np
import jax
import jax.numpy as jnp
from jax import lax
from jax.experimental import pallas as pl
from jax.experimental.pallas import tpu as pltpu

F32 = jnp.float32
BF16 = jnp.bfloat16
EPS = 1e-6

D_MODEL = 2048
LANES = 128
POOL_WINDOWS = (2, 4, 8, 16)
POOL_WIDTH = D_MODEL // 2
POOL_GROUP_DIM = POOL_WIDTH // len(POOL_WINDOWS)
POOL_OUT_DIM = D_MODEL // len(POOL_WINDOWS)
POOL_HISTORY = 16
HG_HEADS = 16
HG_DIM = 128
HG_CHUNK = 64
HG_LEVELS = (32, 16, 8, 4, 2, 1)
PEER_HEADS = 8
PEER_NKEYS = 128
PEER_TOPK = 16
PEER_EXPERTS = PEER_NKEYS * PEER_NKEYS
INV_SQRT2 = 0.7071067811865476

OFF_POOL = 0
OFF_Q = OFF_POOL + POOL_WIDTH
OFF_F = OFF_Q + D_MODEL
OFF_I = OFF_F + D_MODEL
OFF_G = OFF_I + D_MODEL
OFF_GA = OFF_G + D_MODEL
OFF_GB = OFF_GA + D_MODEL
IN_WIDTH = OFF_GB + D_MODEL

VMEM_LIMIT_BYTES = 58 * 1024 * 1024

NT_DIMS = (((1,), (1,)), ((), ()))
TN_DIMS = (((0,), (0,)), ((), ()))


def _tiles(batch, seq):
    tokens = batch * seq
    return dict(
        ada_tn=1024,
        proj_tm=min(1024, seq), proj_tn=1024,
        mix_ts=min(256, seq),
        out_tm=min(512, seq),
        route_tl=LANES,
        peer_tb=min(512, seq), peer_eb=1024,
        tokens=tokens,
    )


def _params(semantics):
    return pltpu.CompilerParams(dimension_semantics=semantics, vmem_limit_bytes=VMEM_LIMIT_BYTES)


def _rms(x):
    return x * lax.rsqrt(jnp.mean(x * x, axis=-1, keepdims=True) + EPS)


def _ada_kernel(c_ref, w_ref, b_ref, o_ref):
    c = c_ref[...]
    cs = c * jax.nn.sigmoid(c)
    o_ref[...] = jnp.dot(cs.astype(BF16), w_ref[...].astype(BF16), preferred_element_type=F32) + b_ref[...]


def _ada(c, w, b, tn):
    bsz, n = c.shape[0], w.shape[1]
    return pl.pallas_call(
        _ada_kernel,
        out_shape=jax.ShapeDtypeStruct((bsz, n), F32),
        grid=(n // tn,),
        in_specs=[pl.BlockSpec((bsz, D_MODEL), lambda j: (0, 0)),
                  pl.BlockSpec((D_MODEL, tn), lambda j: (0, j)),
                  pl.BlockSpec((1, tn), lambda j: (0, j))],
        out_specs=pl.BlockSpec((bsz, tn), lambda j: (0, j)),
        compiler_params=_params(("arbitrary",)),
        name="ada",
    )(c, w, b)


def _proj_kernel(x_ref, ada_ref, g_ref, w_ref, o_ref, h_ref):
    @pl.when(pl.program_id(1) == 0)
    def _():
        y = _rms(x_ref[...]) * g_ref[...]
        h_ref[...] = (y * (1.0 + ada_ref[1:2, :]) + ada_ref[0:1, :]).astype(BF16)

    o_ref[...] = jnp.dot(h_ref[...], w_ref[...], preferred_element_type=F32).astype(o_ref.dtype)


def _proj(x2d, ada3, gain, w_tiles, seq, tm):
    tokens = x2d.shape[0]
    n_tiles, _, tn = w_tiles.shape
    per_batch = seq // tm
    return pl.pallas_call(
        _proj_kernel,
        out_shape=jax.ShapeDtypeStruct((tokens, n_tiles * tn), BF16),
        grid=(tokens // tm, n_tiles),
        in_specs=[pl.BlockSpec((tm, D_MODEL), lambda i, j: (i, 0)),
                  pl.BlockSpec((None, 6, D_MODEL), lambda i, j: (i // per_batch, 0, 0)),
                  pl.BlockSpec((1, D_MODEL), lambda i, j: (0, 0)),
                  pl.BlockSpec((None, D_MODEL, tn), lambda i, j: (j, 0, 0))],
        out_specs=pl.BlockSpec((tm, tn), lambda i, j: (i, j)),
        scratch_shapes=[pltpu.VMEM((tm, D_MODEL), BF16)],
        compiler_params=_params(("arbitrary", "arbitrary")),
        name="proj",
    )(x2d, ada3, gain, w_tiles)


def _decay_sum_matrix():
    c = HG_CHUNK
    groups = []
    for m in HG_LEVELS:
        g = np.zeros((c, c), np.float32)
        for t in range(c):
            ref = (t // (2 * m)) * 2 * m + m - 1
            if (t // m) % 2 == 1:
                g[t, ref + 1:t + 1] = 1.0
            else:
                g[t, t + 1:ref + 1] = 1.0
        groups.append(g)
    groups.append(np.tril(np.ones((c, c), np.float32)))
    groups.append(np.triu(np.ones((c, c), np.float32), 1))
    n = np.concatenate(groups, axis=0)
    return np.concatenate([n, n, n], axis=1)


def _forget_gate(z, lb):
    t = jnp.exp(-jnp.abs(z))
    r = 1.0 / (1.0 + t)
    pos = z >= 0.0
    sig_pos = jnp.where(pos, r, t * r)
    sig_neg = jnp.where(pos, t * r, r)
    return jnp.log(lb + (1.0 - lb) * sig_pos), (1.0 - lb) * sig_neg


def _mixer_kernel(proj_ref, lbl_ref, hgn_ref, pw_ref, ps_ref, n_ref, o_ref,
                  st_ref, pbuf_ref, ya_ref, lf3_ref, kk_ref, e_ref, attn_ref, oi_ref, *, ts):
    sblk = pl.program_id(1)
    hist = POOL_HISTORY

    @pl.when(sblk == 0)
    def _():
        st_ref[...] = jnp.zeros_like(st_ref)
        pbuf_ref[0:hist, :] = jnp.zeros((hist, POOL_WIDTH), F32)

    pbuf_ref[hist:hist + ts, :] = proj_ref[:, OFF_POOL:OFF_POOL + POOL_WIDTH].astype(F32)
    pos1 = (sblk * ts + lax.broadcasted_iota(jnp.int32, (ts, 1), 0) + 1).astype(F32)
    for g, w in enumerate(POOL_WINDOWS):
        cols = slice(g * POOL_GROUP_DIM, (g + 1) * POOL_GROUP_DIM)
        cur = pbuf_ref[hist:hist + ts, cols]
        acc = cur
        for back in range(1, w):
            acc = acc + pbuf_ref[hist - back:hist - back + ts, cols]
        pooled = acc / jnp.minimum(pos1, float(w)) - cur
        oc = slice(g * POOL_OUT_DIM, (g + 1) * POOL_OUT_DIM)
        ya_ref[:, oc] = jnp.dot(pooled.astype(BF16), pw_ref[g], preferred_element_type=F32) * ps_ref[:, oc]
    pbuf_ref[0:hist, :] = pbuf_ref[ts:ts + hist, :]

    lbl = lbl_ref[...]
    lbe = jnp.exp(lbl - jnp.max(lbl, axis=0, keepdims=True))
    lb_all = lbe[0:1, :] / jnp.sum(lbe, axis=0, keepdims=True)

    c = HG_CHUNK
    nmat = n_ref[...]
    row = lax.broadcasted_iota(jnp.int32, (c, HG_DIM), 0)
    row_c = lax.broadcasted_iota(jnp.int32, (c, c), 0)
    col_c = lax.broadcasted_iota(jnp.int32, (c, c), 1)

    def chunk_body(ci, carry):
        rows = pl.ds(pl.multiple_of(ci * c, c), c)

        def seg(off, h):
            return proj_ref[rows, off + h * HG_DIM:off + (h + 1) * HG_DIM].astype(F32)

        for hp in range(0, HG_HEADS, 2):
            for h in (hp, hp + 1):
                hc = slice(h * HG_DIM, (h + 1) * HG_DIM)
                lf, kk = _forget_gate(seg(OFF_F, h), lb_all[:, hc])
                hi = lf.astype(BF16)
                rem = lf - hi.astype(F32)
                mid = rem.astype(BF16)
                lf3_ref[0:c, hc] = hi
                lf3_ref[c:2 * c, hc] = mid
                lf3_ref[2 * c:3 * c, hc] = (rem - mid.astype(F32)).astype(BF16)
                kk_ref[:, hc] = kk
            pc = slice(hp * HG_DIM, (hp + 2) * HG_DIM)
            e_ref[:, pc] = jnp.dot(nmat, lf3_ref[:, pc], preferred_element_type=F32)

        for h in range(HG_HEADS):
            hc = slice(h * HG_DIM, (h + 1) * HG_DIM)
            q, kk, iv = seg(OFF_Q, h), kk_ref[:, hc], seg(OFF_I, h)
            attn = jnp.zeros((c, c), F32)
            for li, m in enumerate(HG_LEVELS):
                right = (row & m) != 0
                xs = jnp.where(right, q, kk) * jnp.exp(e_ref[li * c:(li + 1) * c, hc])
                ql = jnp.where(right, xs, 0.0).astype(BF16)
                kl = jnp.where(right, 0.0, xs).astype(BF16)
                a = lax.dot_general(ql, kl, NT_DIMS, preferred_element_type=F32)
                if 2 * m < c:
                    sh = int(np.log2(2 * m))
                    a = jnp.where((row_c >> sh) == (col_c >> sh), a, 0.0)
                attn = attn + a
            attn_ref[h] = attn.astype(BF16)
            dc = jnp.exp(e_ref[6 * c:7 * c, hc])
            dr = jnp.exp(e_ref[7 * c:8 * c, hc])
            st = st_ref[h]
            carried = lax.dot_general((q * dc).astype(BF16), st.astype(BF16), NT_DIMS,
                                      preferred_element_type=F32)
            oi_ref[:, hc] = carried + jnp.sum(q * kk, axis=-1, keepdims=True) * iv
            upd = lax.dot_general(iv.astype(BF16), (kk * dr).astype(BF16), TN_DIMS,
                                  preferred_element_type=F32)
            st_ref[h] = st * dc[c - 1:c, :] + upd

        for h in range(HG_HEADS):
            hc = slice(h * HG_DIM, (h + 1) * HG_DIM)
            o = jnp.dot(attn_ref[h], seg(OFF_I, h).astype(BF16), preferred_element_type=F32) + oi_ref[:, hc]
            g = seg(OFF_G, h)
            yb = _rms(o) * hgn_ref[:, hc] * (g * jax.nn.sigmoid(g))
            merged = (jax.nn.sigmoid(seg(OFF_GA, h)) * ya_ref[rows, hc]
                      + jax.nn.sigmoid(seg(OFF_GB, h)) * yb)
            o_ref[rows, hc] = merged.astype(o_ref.dtype)
        return carry

    lax.fori_loop(0, ts // c, chunk_body, 0)


def _mixer(proj, lb_logits, hg_norm, pool_w, pool_scale, nmat, batch, seq, ts):
    tokens = proj.shape[0]
    per_batch = seq // ts
    return pl.pallas_call(
        functools.partial(_mixer_kernel, ts=ts),
        out_shape=jax.ShapeDtypeStruct((tokens, D_MODEL), BF16),
        grid=(batch, per_batch),
        in_specs=[pl.BlockSpec((ts, IN_WIDTH), lambda b, s: (b * per_batch + s, 0)),
                  pl.BlockSpec(lb_logits.shape, lambda b, s: (0, 0)),
                  pl.BlockSpec((1, D_MODEL), lambda b, s: (0, 0)),
                  pl.BlockSpec(pool_w.shape, lambda b, s: (0, 0, 0)),
                  pl.BlockSpec((1, D_MODEL), lambda b, s: (0, 0)),
                  pl.BlockSpec(nmat.shape, lambda b, s: (0, 0))],
        out_specs=pl.BlockSpec((ts, D_MODEL), lambda b, s: (b * per_batch + s, 0)),
        scratch_shapes=[pltpu.VMEM((HG_HEADS, HG_DIM, HG_DIM), F32),
                        pltpu.VMEM((POOL_HISTORY + ts, POOL_WIDTH), F32),
                        pltpu.VMEM((ts, D_MODEL), F32),
                        pltpu.VMEM((3 * HG_CHUNK, D_MODEL), BF16),
                        pltpu.VMEM((HG_CHUNK, D_MODEL), F32),
                        pltpu.VMEM((nmat.shape[0], D_MODEL), F32),
                        pltpu.VMEM((HG_HEADS, HG_CHUNK, HG_CHUNK), BF16),
                        pltpu.VMEM((HG_CHUNK, D_MODEL), F32)],
        compiler_params=_params(("arbitrary", "arbitrary")),
        name="mixer",
    )(proj, lb_logits, hg_norm, pool_w, pool_scale, nmat)


def _out_kernel(m_ref, x_ref, ada_ref, wo_ref, g_ref, wq_ref, keys_ref, x1_ref, h2_ref, sc_ref):
    y = jnp.dot(m_ref[...], wo_ref[...], preferred_element_type=F32)
    x1 = x_ref[...] + (1.0 + ada_ref[2:3, :]) * y
    x1_ref[...] = x1
    h2 = ((_rms(x1) * g_ref[...]) * (1.0 + ada_ref[4:5, :]) + ada_ref[3:4, :]).astype(BF16)
    h2_ref[...] = h2
    q = jnp.dot(h2, wq_ref[...], preferred_element_type=F32)
    for hp in range(2 * PEER_HEADS):
        rows = slice(hp * PEER_NKEYS, (hp + 1) * PEER_NKEYS)
        qs = q[:, hp * LANES:(hp + 1) * LANES].astype(BF16)
        sc_ref[rows, :] = lax.dot_general(keys_ref[hp], qs, NT_DIMS, preferred_element_type=F32)


def _out(merged, x2d, ada3, w_out, gain, w_q, keys, seq, tm):
    tokens = x2d.shape[0]
    per_batch = seq // tm
    nsc = 2 * PEER_HEADS * PEER_NKEYS
    return pl.pallas_call(
        _out_kernel,
        out_shape=(jax.ShapeDtypeStruct((tokens, D_MODEL), F32),
                   jax.ShapeDtypeStruct((tokens, D_MODEL), BF16),
                   jax.ShapeDtypeStruct((nsc, tokens), F32)),
        grid=(tokens // tm,),
        in_specs=[pl.BlockSpec((tm, D_MODEL), lambda i: (i, 0)),
                  pl.BlockSpec((tm, D_MODEL), lambda i: (i, 0)),
                  pl.BlockSpec((None, 6, D_MODEL), lambda i: (i // per_batch, 0, 0)),
                  pl.BlockSpec((D_MODEL, D_MODEL), lambda i: (0, 0), pipeline_mode=pl.Buffered(1)),
                  pl.BlockSpec((1, D_MODEL), lambda i: (0, 0)),
                  pl.BlockSpec(w_q.shape, lambda i: (0, 0), pipeline_mode=pl.Buffered(1)),
                  pl.BlockSpec(keys.shape, lambda i: (0, 0, 0))],
        out_specs=(pl.BlockSpec((tm, D_MODEL), lambda i: (i, 0)),
                   pl.BlockSpec((tm, D_MODEL), lambda i: (i, 0)),
                   pl.BlockSpec((nsc, tm), lambda i: (0, i))),
        compiler_params=_params(("arbitrary",)),
        name="out",
    )(merged, x2d, ada3, w_out, gain, w_q, keys)


SUBLANES = 8


def _sort_pairs(n):
    pairs = []
    p = 1
    while p < n:
        k = p
        while k >= 1:
            for j in range(k % p, n - k, 2 * k):
                for i in range(min(k, n - j - k)):
                    if (i + j) // (2 * p) == (i + j + k) // (2 * p):
                        pairs.append((i + j, i + j + k))
            k //= 2
        p *= 2
    return pairs


def _compare_exchange(vals, i, j):
    a, b = vals[i], vals[j]
    if b is None:
        return
    if a is None:
        vals[i], vals[j] = b, None
        return
    vals[i], vals[j] = jnp.maximum(a, b), jnp.minimum(a, b)


def _top16_sorted(blocks):
    k = PEER_TOPK
    vals = list(blocks) + [None] * (k - len(blocks))
    for i, j in _sort_pairs(k):
        _compare_exchange(vals, i, j)
    shift = SUBLANES // 2
    while shift >= 1:
        other = [None if v is None else pltpu.roll(v, shift, axis=0) for v in vals]
        merged = []
        for i in range(k):
            a, b = vals[i], other[k - 1 - i]
            merged.append(b if a is None else a if b is None else jnp.maximum(a, b))
        vals = merged
        d = k // 2
        while d >= 1:
            for i in range(k):
                if i & d == 0:
                    _compare_exchange(vals, i, i + d)
            d //= 2
        shift //= 2
    return vals


def _route_kernel(sc_ref, r1_ref, b_ref, n_ref, a_ref):
    tl = sc_ref.shape[1]
    k = PEER_TOPK
    nblk = PEER_NKEYS // SUBLANES
    sub = lax.broadcasted_iota(jnp.int32, (SUBLANES, tl), 0)

    def rows_from(vals):
        out = vals[0]
        for r in range(1, SUBLANES):
            out = jnp.where(sub == r, vals[r], out)
        return out

    for h in range(PEER_HEADS):
        base0, base1 = (2 * h) * PEER_NKEYS, (2 * h + 1) * PEER_NKEYS
        s0 = [sc_ref[base0 + q * SUBLANES:base0 + (q + 1) * SUBLANES, :] for q in range(nblk)]
        s1 = [sc_ref[base1 + q * SUBLANES:base1 + (q + 1) * SUBLANES, :] for q in range(nblk)]
        v0 = _top16_sorted(s0)
        v1 = _top16_sorted(s1)
        v1_lo, v1_hi, v0_hi = rows_from(v1[0:8]), rows_from(v1[8:16]), rows_from(v0[8:16])
        cands = [v0[0] + v1_lo, v0[0] + v1_hi]
        cands += [v0[a] + v1_lo for a in range(1, 8)]
        cands += [v0_hi + v1[0]]
        tau = _top16_sorted(cands)[k - 1]
        top = v0[0] + v1[0]
        sel = [cc >= tau for cc in cands]
        z = functools.reduce(jnp.add, [jnp.where(ss, jnp.exp(cc - top), 0.0) for ss, cc in zip(sel, cands)])
        z = jnp.sum(z, axis=0, keepdims=True)
        cnts = [jnp.where(ss, 1.0, 0.0) for ss in sel]
        n_rows = [jnp.sum(cnts[0] + cnts[1], axis=0, keepdims=True)]
        n_rows += [jnp.sum(cnts[a + 1], axis=0, keepdims=True) for a in range(1, 8)]
        n_rows += [cnts[9][a:a + 1, :] for a in range(8)]
        inv_z = INV_SQRT2 / z
        for q in range(0, nblk, 2):
            rank, count = [], []
            for qq in (q, q + 1):
                r = jnp.zeros((SUBLANES, tl), F32)
                for kk in range(k):
                    r = r + jnp.where(v1[kk] > s1[qq], 1.0, 0.0)
                rank.append(r)
                c = jnp.zeros((SUBLANES, tl), F32)
                for a in range(k):
                    c = jnp.where(s0[qq] == v0[a], n_rows[a], c)
                count.append(c)
            rows = slice(q * SUBLANES, (q + 2) * SUBLANES)
            r1_ref[h, rows, :] = jnp.concatenate(rank, axis=0).astype(r1_ref.dtype)
            gate1 = jnp.exp(jnp.concatenate([s1[q], s1[q + 1]], axis=0) - v1[0][0:1]) * inv_z
            b_ref[h, rows, :] = gate1.astype(b_ref.dtype)
            n_ref[h, rows, :] = jnp.concatenate(count, axis=0)
            a_ref[h, rows, :] = jnp.exp(jnp.concatenate([s0[q], s0[q + 1]], axis=0) - v0[0][0:1])


def _route(scores_t, tl):
    tokens = scores_t.shape[1]
    shp = (PEER_HEADS, PEER_NKEYS, tokens)
    blk = pl.BlockSpec((PEER_HEADS, PEER_NKEYS, tl), lambda i: (0, 0, i))
    return pl.pallas_call(
        _route_kernel,
        out_shape=(jax.ShapeDtypeStruct(shp, BF16), jax.ShapeDtypeStruct(shp, BF16),
                   jax.ShapeDtypeStruct(shp, F32), jax.ShapeDtypeStruct(shp, F32)),
        grid=(tokens // tl,),
        in_specs=[pl.BlockSpec((scores_t.shape[0], tl), lambda i: (0, i))],
        out_specs=(blk, blk, blk, blk),
        compiler_params=_params(("arbitrary",)),
        name="route",
    )(scores_t)


def _peer_kernel(h2_ref, u_ref, vt_ref, r1_ref, b_ref, n_ref, a_ref, x1_ref, ada_ref, fg_ref, o_ref,
                 hid_ref, p_ref, acc_ref, *, tb, eb):
    e = pl.program_id(1)
    sub = 16

    @pl.when(e == 0)
    def _():
        acc_ref[...] = jnp.zeros_like(acc_ref)

    hid_ref[...] = lax.dot_general(u_ref[...], h2_ref[...], NT_DIMS, preferred_element_type=F32)

    for ii in range(eb // PEER_NKEYS):
        for lc in range(tb // LANES):
            lanes = slice(lc * LANES, (lc + 1) * LANES)
            n_h = [jnp.broadcast_to(n_ref[h, ii:ii + 1, lanes], (sub, LANES)).astype(BF16)
                   for h in range(PEER_HEADS)]
            a_h = [jnp.broadcast_to(a_ref[h, ii:ii + 1, lanes], (sub, LANES)).astype(BF16)
                   for h in range(PEER_HEADS)]
            for g in range(PEER_NKEYS // sub):
                jr = slice(g * sub, (g + 1) * sub)
                rr = slice(ii * PEER_NKEYS + g * sub, ii * PEER_NKEYS + (g + 1) * sub)
                gate = None
                for h in range(PEER_HEADS):
                    term = jnp.where(r1_ref[h, jr, lanes] < n_h[h], b_ref[h, jr, lanes],
                                     jnp.zeros((), BF16)) * a_h[h]
                    gate = term if gate is None else gate + term
                w = hid_ref[rr, lanes].astype(BF16)
                p_ref[rr, lanes] = gate * (w * (1.0 + lax.erf(w)))
    acc_ref[...] += jnp.dot(vt_ref[...], p_ref[...], preferred_element_type=F32)

    @pl.when(e == pl.num_programs(1) - 1)
    def _():
        x2 = x1_ref[...] + (1.0 + ada_ref[5:6, :]) * acc_ref[...].T
        o_ref[...] = _rms(x2) * fg_ref[...]


def _peer(h2, u, vt, r1t, bt, nt, at, x1, ada3, final_gain, seq, tb, eb):
    tokens = h2.shape[0]
    per_batch = seq // tb
    keys_per_tile = eb // PEER_NKEYS
    rt_blk = pl.BlockSpec((PEER_HEADS, PEER_NKEYS, tb), lambda i, e: (0, 0, i))
    key_blk = pl.BlockSpec((PEER_HEADS, keys_per_tile, tb), lambda i, e: (0, e, i))
    return pl.pallas_call(
        functools.partial(_peer_kernel, tb=tb, eb=eb),
        out_shape=jax.ShapeDtypeStruct((tokens, D_MODEL), F32),
        grid=(tokens // tb, PEER_EXPERTS // eb),
        in_specs=[pl.BlockSpec((tb, D_MODEL), lambda i, e: (i, 0)),
                  pl.BlockSpec((eb, D_MODEL), lambda i, e: (e, 0)),
                  pl.BlockSpec((None, D_MODEL, eb), lambda i, e: (e, 0, 0)),
                  rt_blk, rt_blk, key_blk, key_blk,
                  pl.BlockSpec((tb, D_MODEL), lambda i, e: (i, 0)),
                  pl.BlockSpec((None, 6, D_MODEL), lambda i, e: (i // per_batch, 0, 0)),
                  pl.BlockSpec((1, D_MODEL), lambda i, e: (0, 0))],
        out_specs=pl.BlockSpec((tb, D_MODEL), lambda i, e: (i, 0)),
        scratch_shapes=[pltpu.VMEM((eb, tb), F32),
                        pltpu.VMEM((eb, tb), BF16),
                        pltpu.VMEM((D_MODEL, tb), F32)],
        compiler_params=_params(("arbitrary", "arbitrary")),
        name="peer",
    )(h2, u, vt, r1t, bt, nt, at, x1, ada3, final_gain)


def kernel(x, c, w_ada, b_ada, norm1, w_in, pool_w, pool_scale, lb_logits, hg_norm, w_out,
           norm2, peer_wq, peer_keys, peer_u, peer_v, final_norm):
    batch, seq, d = x.shape
    assert d == D_MODEL and w_ada.shape[0] == 1 and lb_logits.shape[0] == 2
    assert seq % HG_CHUNK == 0 and seq >= POOL_HISTORY
    t = _tiles(batch, seq)
    x2d = x.reshape(batch * seq, d)
    row = lambda v: v.reshape(1, -1).astype(F32)

    ada3 = _ada(c, w_ada[0], b_ada[0].reshape(1, -1), t["ada_tn"]).reshape(batch, 6, d)
    tn = t["proj_tn"]
    w_in_tiles = w_in[0].reshape(d, IN_WIDTH // tn, tn).transpose(1, 0, 2).astype(BF16)
    proj = _proj(x2d, ada3, row(norm1[0]), w_in_tiles, seq, t["proj_tm"])
    merged = _mixer(proj, lb_logits, row(hg_norm[0]), pool_w[0].astype(BF16), row(pool_scale[0]),
                    jnp.asarray(_decay_sum_matrix(), BF16), batch, seq, t["mix_ts"])
    keys = peer_keys[0].reshape(2 * PEER_HEADS, PEER_NKEYS, -1).astype(BF16)
    x1, h2, scores_t = _out(merged, x2d, ada3, w_out[0].astype(BF16), row(norm2[0]),
                            peer_wq[0].astype(BF16), keys, seq, t["out_tm"])
    r1t, bt, nt, at = _route(scores_t, t["route_tl"])
    eb = t["peer_eb"]
    u_scaled = (peer_u[0] * INV_SQRT2).astype(BF16)
    vt = peer_v[0].reshape(PEER_EXPERTS // eb, eb, d).transpose(0, 2, 1).astype(BF16)
    y = _peer(h2, u_scaled, vt, r1t, bt, nt, at, x1, ada3, row(final_norm), seq, t["peer_tb"], eb)
    return y.reshape(batch, seq, d)
```

```python
import functools

import numpy as np
import jax
import jax.numpy as jnp
from jax import lax
from jax.experimental import pallas as pl
from jax.experimental.pallas import tpu as pltpu

F32 = jnp.float32
BF16 = jnp.bfloat16
EPS = 1e-6

D_MODEL = 2048
LANES = 128
POOL_WINDOWS = (2, 4, 8, 16)
POOL_WIDTH = D_MODEL // 2
POOL_GROUP_DIM = POOL_WIDTH // len(POOL_WINDOWS)
POOL_OUT_DIM = D_MODEL // len(POOL_WINDOWS)
POOL_HISTORY = 16
HG_HEADS = 16
HG_DIM = 128
HG_CHUNK = 64
HG_LEVELS = (32, 16, 8, 4, 2, 1)
PEER_HEADS = 8
PEER_NKEYS = 128
PEER_TOPK = 16
PEER_EXPERTS = PEER_NKEYS * PEER_NKEYS
INV_SQRT2 = 0.7071067811865476

OFF_POOL = 0
OFF_Q = OFF_POOL + POOL_WIDTH
OFF_F = OFF_Q + D_MODEL
OFF_I = OFF_F + D_MODEL
OFF_G = OFF_I + D_MODEL
OFF_GA = OFF_G + D_MODEL
OFF_GB = OFF_GA + D_MODEL
IN_WIDTH = OFF_GB + D_MODEL

VMEM_LIMIT_BYTES = 58 * 1024 * 1024

NT_DIMS = (((1,), (1,)), ((), ()))
TN_DIMS = (((0,), (0,)), ((), ()))


def _tiles(batch, seq):
    tokens = batch * seq
    return dict(
        ada_tn=1024,
        proj_tm=min(1024, seq), proj_tn=1024,
        mix_ts=min(256, seq),
        out_tm=min(512, seq),
        route_tl=LANES,
        peer_tb=min(512, seq), peer_eb=1024,
        tokens=tokens,
    )


def _params(semantics):
    return pltpu.CompilerParams(dimension_semantics=semantics, vmem_limit_bytes=VMEM_LIMIT_BYTES)


def _rms(x):
    return x * lax.rsqrt(jnp.mean(x * x, axis=-1, keepdims=True) + EPS)


def _ada_kernel(c_ref, w_ref, b_ref, o_ref):
    c = c_ref[...]
    cs = c * jax.nn.sigmoid(c)
    o_ref[...] = jnp.dot(cs.astype(BF16), w_ref[...].astype(BF16), preferred_element_type=F32) + b_ref[...]


def _ada(c, w, b, tn):
    bsz, n = c.shape[0], w.shape[1]
    return pl.pallas_call(
        _ada_kernel,
        out_shape=jax.ShapeDtypeStruct((bsz, n), F32),
        grid=(n // tn,),
        in_specs=[pl.BlockSpec((bsz, D_MODEL), lambda j: (0, 0)),
                  pl.BlockSpec((D_MODEL, tn), lambda j: (0, j)),
                  pl.BlockSpec((1, tn), lambda j: (0, j))],
        out_specs=pl.BlockSpec((bsz, tn), lambda j: (0, j)),
        compiler_params=_params(("arbitrary",)),
        name="ada",
    )(c, w, b)


def _proj_kernel(x_ref, ada_ref, g_ref, w_ref, o_ref, h_ref):
    @pl.when(pl.program_id(1) == 0)
    def _():
        y = _rms(x_ref[...]) * g_ref[...]
        h_ref[...] = (y * (1.0 + ada_ref[1:2, :]) + ada_ref[0:1, :]).astype(BF16)

    o_ref[...] = jnp.dot(h_ref[...], w_ref[...], preferred_element_type=F32).astype(o_ref.dtype)


def _proj(x2d, ada3, gain, w, seq, tm, tn):
    tokens, n = x2d.shape[0], w.shape[1]
    per_batch = seq // tm
    return pl.pallas_call(
        _proj_kernel,
        out_shape=jax.ShapeDtypeStruct((tokens, n), BF16),
        grid=(tokens // tm, n // tn),
        in_specs=[pl.BlockSpec((tm, D_MODEL), lambda i, j: (i, 0)),
                  pl.BlockSpec((None, 6, D_MODEL), lambda i, j: (i // per_batch, 0, 0)),
                  pl.BlockSpec((1, D_MODEL), lambda i, j: (0, 0)),
                  pl.BlockSpec((D_MODEL, tn), lambda i, j: (0, j))],
        out_specs=pl.BlockSpec((tm, tn), lambda i, j: (i, j)),
        scratch_shapes=[pltpu.VMEM((tm, D_MODEL), BF16)],
        compiler_params=_params(("arbitrary", "arbitrary")),
        name="proj",
    )(x2d, ada3, gain, w)


def _decay_sum_matrix():
    c = HG_CHUNK
    groups = []
    for m in HG_LEVELS:
        g = np.zeros((c, c), np.float32)
        for t in range(c):
            ref = (t // (2 * m)) * 2 * m + m - 1
            if (t // m) % 2 == 1:
                g[t, ref + 1:t + 1] = 1.0
            else:
                g[t, t + 1:ref + 1] = 1.0
        groups.append(g)
    groups.append(np.tril(np.ones((c, c), np.float32)))
    groups.append(np.triu(np.ones((c, c), np.float32), 1))
    n = np.concatenate(groups, axis=0)
    return np.concatenate([n, n, n], axis=1)


def _forget_gate(z, lb):
    t = jnp.exp(-jnp.abs(z))
    r = 1.0 / (1.0 + t)
    pos = z >= 0.0
    sig_pos = jnp.where(pos, r, t * r)
    sig_neg = jnp.where(pos, t * r, r)
    return jnp.log(lb + (1.0 - lb) * sig_pos), (1.0 - lb) * sig_neg


def _mixer_kernel(proj_ref, lbl_ref, hgn_ref, pw_ref, ps_ref, n_ref, o_ref,
                  st_ref, pbuf_ref, ya_ref, lf3_ref, kk_ref, e_ref, attn_ref, oi_ref, *, ts):
    sblk = pl.program_id(1)
    hist = POOL_HISTORY

    @pl.when(sblk == 0)
    def _():
        st_ref[...] = jnp.zeros_like(st_ref)
        pbuf_ref[0:hist, :] = jnp.zeros((hist, POOL_WIDTH), F32)

    pbuf_ref[hist:hist + ts, :] = proj_ref[:, OFF_POOL:OFF_POOL + POOL_WIDTH].astype(F32)
    pos1 = (sblk * ts + lax.broadcasted_iota(jnp.int32, (ts, 1), 0) + 1).astype(F32)
    for g, w in enumerate(POOL_WINDOWS):
        cols = slice(g * POOL_GROUP_DIM, (g + 1) * POOL_GROUP_DIM)
        cur = pbuf_ref[hist:hist + ts, cols]
        acc = cur
        for back in range(1, w):
            acc = acc + pbuf_ref[hist - back:hist - back + ts, cols]
        pooled = acc / jnp.minimum(pos1, float(w)) - cur
        oc = slice(g * POOL_OUT_DIM, (g + 1) * POOL_OUT_DIM)
        ya_ref[:, oc] = jnp.dot(pooled.astype(BF16), pw_ref[g], preferred_element_type=F32) * ps_ref[:, oc]
    pbuf_ref[0:hist, :] = pbuf_ref[ts:ts + hist, :]

    lbl = lbl_ref[...]
    lbe = jnp.exp(lbl - jnp.max(lbl, axis=0, keepdims=True))
    lb_all = lbe[0:1, :] / jnp.sum(lbe, axis=0, keepdims=True)

    c = HG_CHUNK
    nmat = n_ref[...]
    row = lax.broadcasted_iota(jnp.int32, (c, HG_DIM), 0)
    row_c = lax.broadcasted_iota(jnp.int32, (c, c), 0)
    col_c = lax.broadcasted_iota(jnp.int32, (c, c), 1)

    def chunk_body(ci, carry):
        rows = pl.ds(pl.multiple_of(ci * c, c), c)

        def seg(off, h):
            return proj_ref[rows, off + h * HG_DIM:off + (h + 1) * HG_DIM].astype(F32)

        for hp in range(0, HG_HEADS, 2):
            for h in (hp, hp + 1):
                hc = slice(h * HG_DIM, (h + 1) * HG_DIM)
                lf, kk = _forget_gate(seg(OFF_F, h), lb_all[:, hc])
                hi = lf.astype(BF16)
                rem = lf - hi.astype(F32)
                mid = rem.astype(BF16)
                lf3_ref[0:c, hc] = hi
                lf3_ref[c:2 * c, hc] = mid
                lf3_ref[2 * c:3 * c, hc] = (rem - mid.astype(F32)).astype(BF16)
                kk_ref[:, hc] = kk
            pc = slice(hp * HG_DIM, (hp + 2) * HG_DIM)
            e_ref[:, pc] = jnp.dot(nmat, lf3_ref[:, pc], preferred_element_type=F32)

        for h in range(HG_HEADS):
            hc = slice(h * HG_DIM, (h + 1) * HG_DIM)
            q, kk, iv = seg(OFF_Q, h), kk_ref[:, hc], seg(OFF_I, h)
            attn = jnp.zeros((c, c), F32)
            for li, m in enumerate(HG_LEVELS):
                right = (row & m) != 0
                xs = jnp.where(right, q, kk) * jnp.exp(e_ref[li * c:(li + 1) * c, hc])
                ql = jnp.where(right, xs, 0.0).astype(BF16)
                kl = jnp.where(right, 0.0, xs).astype(BF16)
                a = lax.dot_general(ql, kl, NT_DIMS, preferred_element_type=F32)
                if 2 * m < c:
                    sh = int(np.log2(2 * m))
                    a = jnp.where((row_c >> sh) == (col_c >> sh), a, 0.0)
                attn = attn + a
            attn_ref[h] = attn.astype(BF16)
            dc = jnp.exp(e_ref[6 * c:7 * c, hc])
            dr = jnp.exp(e_ref[7 * c:8 * c, hc])
            st = st_ref[h]
            carried = lax.dot_general((q * dc).astype(BF16), st.astype(BF16), NT_DIMS,
                                      preferred_element_type=F32)
            oi_ref[:, hc] = carried + jnp.sum(q * kk, axis=-1, keepdims=True) * iv
            upd = lax.dot_general(iv.astype(BF16), (kk * dr).astype(BF16), TN_DIMS,
                                  preferred_element_type=F32)
            st_ref[h] = st * dc[c - 1:c, :] + upd

        for h in range(HG_HEADS):
            hc = slice(h * HG_DIM, (h + 1) * HG_DIM)
            o = jnp.dot(attn_ref[h], seg(OFF_I, h).astype(BF16), preferred_element_type=F32) + oi_ref[:, hc]
            g = seg(OFF_G, h)
            yb = _rms(o) * hgn_ref[:, hc] * (g * jax.nn.sigmoid(g))
            merged = (jax.nn.sigmoid(seg(OFF_GA, h)) * ya_ref[rows, hc]
                      + jax.nn.sigmoid(seg(OFF_GB, h)) * yb)
            o_ref[rows, hc] = merged.astype(o_ref.dtype)
        return carry

    lax.fori_loop(0, ts // c, chunk_body, 0)


def _mixer(proj, lb_logits, hg_norm, pool_w, pool_scale, nmat, batch, seq, ts):
    tokens = proj.shape[0]
    per_batch = seq // ts
    return pl.pallas_call(
        functools.partial(_mixer_kernel, ts=ts),
        out_shape=jax.ShapeDtypeStruct((tokens, D_MODEL), BF16),
        grid=(batch, per_batch),
        in_specs=[pl.BlockSpec((ts, IN_WIDTH), lambda b, s: (b * per_batch + s, 0)),
                  pl.BlockSpec(lb_logits.shape, lambda b, s: (0, 0)),
                  pl.BlockSpec((1, D_MODEL), lambda b, s: (0, 0)),
                  pl.BlockSpec(pool_w.shape, lambda b, s: (0, 0, 0)),
                  pl.BlockSpec((1, D_MODEL), lambda b, s: (0, 0)),
                  pl.BlockSpec(nmat.shape, lambda b, s: (0, 0))],
        out_specs=pl.BlockSpec((ts, D_MODEL), lambda b, s: (b * per_batch + s, 0)),
        scratch_shapes=[pltpu.VMEM((HG_HEADS, HG_DIM, HG_DIM), F32),
                        pltpu.VMEM((POOL_HISTORY + ts, POOL_WIDTH), F32),
                        pltpu.VMEM((ts, D_MODEL), F32),
                        pltpu.VMEM((3 * HG_CHUNK, D_MODEL), BF16),
                        pltpu.VMEM((HG_CHUNK, D_MODEL), F32),
                        pltpu.VMEM((nmat.shape[0], D_MODEL), F32),
                        pltpu.VMEM((HG_HEADS, HG_CHUNK, HG_CHUNK), BF16),
                        pltpu.VMEM((HG_CHUNK, D_MODEL), F32)],
        compiler_params=_params(("arbitrary", "arbitrary")),
        name="mixer",
    )(proj, lb_logits, hg_norm, pool_w, pool_scale, nmat)


def _out_kernel(m_ref, x_ref, ada_ref, wo_ref, g_ref, wq_ref, keys_ref, x1_ref, h2_ref, sc_ref):
    y = jnp.dot(m_ref[...], wo_ref[...], preferred_element_type=F32)
    x1 = x_ref[...] + (1.0 + ada_ref[2:3, :]) * y
    x1_ref[...] = x1
    h2 = ((_rms(x1) * g_ref[...]) * (1.0 + ada_ref[4:5, :]) + ada_ref[3:4, :]).astype(BF16)
    h2_ref[...] = h2
    q = jnp.dot(h2, wq_ref[...], preferred_element_type=F32)
    for hp in range(2 * PEER_HEADS):
        rows = slice(hp * PEER_NKEYS, (hp + 1) * PEER_NKEYS)
        qs = q[:, hp * LANES:(hp + 1) * LANES].astype(BF16)
        sc_ref[rows, :] = lax.dot_general(keys_ref[hp], qs, NT_DIMS, preferred_element_type=F32)


def _out(merged, x2d, ada3, w_out, gain, w_q, keys, seq, tm):
    tokens = x2d.shape[0]
    per_batch = seq // tm
    nsc = 2 * PEER_HEADS * PEER_NKEYS
    return pl.pallas_call(
        _out_kernel,
        out_shape=(jax.ShapeDtypeStruct((tokens, D_MODEL), F32),
                   jax.ShapeDtypeStruct((tokens, D_MODEL), BF16),
                   jax.ShapeDtypeStruct((nsc, tokens), F32)),
        grid=(tokens // tm,),
        in_specs=[pl.BlockSpec((tm, D_MODEL), lambda i: (i, 0)),
                  pl.BlockSpec((tm, D_MODEL), lambda i: (i, 0)),
                  pl.BlockSpec((None, 6, D_MODEL), lambda i: (i // per_batch, 0, 0)),
                  pl.BlockSpec((D_MODEL, D_MODEL), lambda i: (0, 0), pipeline_mode=pl.Buffered(1)),
                  pl.BlockSpec((1, D_MODEL), lambda i: (0, 0)),
                  pl.BlockSpec(w_q.shape, lambda i: (0, 0), pipeline_mode=pl.Buffered(1)),
                  pl.BlockSpec(keys.shape, lambda i: (0, 0, 0))],
        out_specs=(pl.BlockSpec((tm, D_MODEL), lambda i: (i, 0)),
                   pl.BlockSpec((tm, D_MODEL), lambda i: (i, 0)),
                   pl.BlockSpec((nsc, tm), lambda i: (0, i))),
        compiler_params=_params(("arbitrary",)),
        name="out",
    )(merged, x2d, ada3, w_out, gain, w_q, keys)


SUBLANES = 8


def _sort_pairs(n):
    pairs = []
    p = 1
    while p < n:
        k = p
        while k >= 1:
            for j in range(k % p, n - k, 2 * k):
                for i in range(min(k, n - j - k)):
                    if (i + j) // (2 * p) == (i + j + k) // (2 * p):
                        pairs.append((i + j, i + j + k))
            k //= 2
        p *= 2
    return pairs


def _compare_exchange(vals, i, j):
    a, b = vals[i], vals[j]
    if b is None:
        return
    if a is None:
        vals[i], vals[j] = b, None
        return
    vals[i], vals[j] = jnp.maximum(a, b), jnp.minimum(a, b)


def _top16_sorted(blocks):
    k = PEER_TOPK
    vals = list(blocks) + [None] * (k - len(blocks))
    for i, j in _sort_pairs(k):
        _compare_exchange(vals, i, j)
    shift = SUBLANES // 2
    while shift >= 1:
        other = [None if v is None else pltpu.roll(v, shift, axis=0) for v in vals]
        merged = []
        for i in range(k):
            a, b = vals[i], other[k - 1 - i]
            merged.append(b if a is None else a if b is None else jnp.maximum(a, b))
        vals = merged
        d = k // 2
        while d >= 1:
            for i in range(k):
                if i & d == 0:
                    _compare_exchange(vals, i, i + d)
            d //= 2
        shift //= 2
    return vals


def _route_kernel(sc_ref, r1_ref, b_ref, n_ref, a_ref):
    tl = sc_ref.shape[1]
    k = PEER_TOPK
    nblk = PEER_NKEYS // SUBLANES
    sub = lax.broadcasted_iota(jnp.int32, (SUBLANES, tl), 0)

    def rows_from(vals):
        out = vals[0]
        for r in range(1, SUBLANES):
            out = jnp.where(sub == r, vals[r], out)
        return out

    for h in range(PEER_HEADS):
        base0, base1 = (2 * h) * PEER_NKEYS, (2 * h + 1) * PEER_NKEYS
        s0 = [sc_ref[base0 + q * SUBLANES:base0 + (q + 1) * SUBLANES, :] for q in range(nblk)]
        s1 = [sc_ref[base1 + q * SUBLANES:base1 + (q + 1) * SUBLANES, :] for q in range(nblk)]
        v0 = _top16_sorted(s0)
        v1 = _top16_sorted(s1)
        v1_lo, v1_hi, v0_hi = rows_from(v1[0:8]), rows_from(v1[8:16]), rows_from(v0[8:16])
        cands = [v0[0] + v1_lo, v0[0] + v1_hi]
        cands += [v0[a] + v1_lo for a in range(1, 8)]
        cands += [v0_hi + v1[0]]
        tau = _top16_sorted(cands)[k - 1]
        top = v0[0] + v1[0]
        sel = [cc >= tau for cc in cands]
        z = functools.reduce(jnp.add, [jnp.where(ss, jnp.exp(cc - top), 0.0) for ss, cc in zip(sel, cands)])
        z = jnp.sum(z, axis=0, keepdims=True)
        cnts = [jnp.where(ss, 1.0, 0.0) for ss in sel]
        n_rows = [jnp.sum(cnts[0] + cnts[1], axis=0, keepdims=True)]
        n_rows += [jnp.sum(cnts[a + 1], axis=0, keepdims=True) for a in range(1, 8)]
        n_rows += [cnts[9][a:a + 1, :] for a in range(8)]
        inv_z = INV_SQRT2 / z
        for q in range(0, nblk, 2):
            rank, count = [], []
            for qq in (q, q + 1):
                r = jnp.zeros((SUBLANES, tl), F32)
                for kk in range(k):
                    r = r + jnp.where(v1[kk] > s1[qq], 1.0, 0.0)
                rank.append(r)
                c = jnp.zeros((SUBLANES, tl), F32)
                for a in range(k):
                    c = jnp.where(s0[qq] == v0[a], n_rows[a], c)
                count.append(c)
            rows = slice(q * SUBLANES, (q + 2) * SUBLANES)
            packed = slice((h * PEER_NKEYS + q * SUBLANES) // 2, (h * PEER_NKEYS + (q + 2) * SUBLANES) // 2)
            r1_ref[packed, :] = pltpu.bitcast(jnp.concatenate(rank, axis=0).astype(BF16), jnp.uint32)
            gate1 = jnp.exp(jnp.concatenate([s1[q], s1[q + 1]], axis=0) - v1[0][0:1]) * inv_z
            b_ref[packed, :] = pltpu.bitcast(gate1.astype(BF16), jnp.uint32)
            n_ref[h, rows, :] = jnp.concatenate(count, axis=0)
            a_ref[h, rows, :] = jnp.exp(jnp.concatenate([s0[q], s0[q + 1]], axis=0) - v0[0][0:1])


def _route(scores_t, tl):
    tokens = scores_t.shape[1]
    shp = (PEER_HEADS, PEER_NKEYS, tokens)
    blk = pl.BlockSpec((PEER_HEADS, PEER_NKEYS, tl), lambda i: (0, 0, i))
    flat_shp = (PEER_HEADS * PEER_NKEYS // 2, tokens)
    flat_blk = pl.BlockSpec((PEER_HEADS * PEER_NKEYS // 2, tl), lambda i: (0, i))
    return pl.pallas_call(
        _route_kernel,
        out_shape=(jax.ShapeDtypeStruct(flat_shp, jnp.uint32), jax.ShapeDtypeStruct(flat_shp, jnp.uint32),
                   jax.ShapeDtypeStruct(shp, F32), jax.ShapeDtypeStruct(shp, F32)),
        grid=(tokens // tl,),
        in_specs=[pl.BlockSpec((scores_t.shape[0], tl), lambda i: (0, i))],
        out_specs=(flat_blk, flat_blk, blk, blk),
        compiler_params=_params(("arbitrary",)),
        name="route",
    )(scores_t)


def _peer_kernel(h2_ref, u_ref, vt_ref, r1_ref, b_ref, n_ref, a_ref, x1_ref, ada_ref, fg_ref, o_ref,
                 hid_ref, p_ref, acc_ref, *, tb, eb):
    e = pl.program_id(1)
    sub = 16

    @pl.when(e == 0)
    def _():
        acc_ref[...] = jnp.zeros_like(acc_ref)

    hid_ref[...] = lax.dot_general(u_ref[...], h2_ref[...], NT_DIMS, preferred_element_type=F32)

    for ii in range(eb // PEER_NKEYS):
        for lc in range(tb // LANES):
            lanes = slice(lc * LANES, (lc + 1) * LANES)
            n_h = [jnp.broadcast_to(n_ref[h, ii:ii + 1, lanes], (sub, LANES)).astype(BF16)
                   for h in range(PEER_HEADS)]
            a_h = [jnp.broadcast_to(a_ref[h, ii:ii + 1, lanes], (sub, LANES)).astype(BF16)
                   for h in range(PEER_HEADS)]
            for g in range(PEER_NKEYS // sub):
                rr = slice(ii * PEER_NKEYS + g * sub, ii * PEER_NKEYS + (g + 1) * sub)
                gate = None
                for h in range(PEER_HEADS):
                    jr = slice((h * PEER_NKEYS + g * sub) // 2, (h * PEER_NKEYS + (g + 1) * sub) // 2)
                    rank = pltpu.bitcast(r1_ref[jr, lanes], BF16)
                    factor = pltpu.bitcast(b_ref[jr, lanes], BF16)
                    term = jnp.where(rank < n_h[h], factor, jnp.zeros((), BF16)) * a_h[h]
                    gate = term if gate is None else gate + term
                w = hid_ref[rr, lanes].astype(BF16)
                p_ref[rr, lanes] = gate * (w * (1.0 + lax.erf(w)))
    acc_ref[...] += jnp.dot(vt_ref[...], p_ref[...], preferred_element_type=F32)

    @pl.when(e == pl.num_programs(1) - 1)
    def _():
        x2 = x1_ref[...] + (1.0 + ada_ref[5:6, :]) * acc_ref[...].T
        o_ref[...] = _rms(x2) * fg_ref[...]


def _peer(h2, u, vt, r1t, bt, nt, at, x1, ada3, final_gain, seq, tb, eb):
    tokens = h2.shape[0]
    per_batch = seq // tb
    keys_per_tile = eb // PEER_NKEYS
    rt_blk = pl.BlockSpec((PEER_HEADS * PEER_NKEYS // 2, tb), lambda i, e: (0, i))
    key_blk = pl.BlockSpec((PEER_HEADS, keys_per_tile, tb), lambda i, e: (0, e, i))
    return pl.pallas_call(
        functools.partial(_peer_kernel, tb=tb, eb=eb),
        out_shape=jax.ShapeDtypeStruct((tokens, D_MODEL), F32),
        grid=(tokens // tb, PEER_EXPERTS // eb),
        in_specs=[pl.BlockSpec((tb, D_MODEL), lambda i, e: (i, 0)),
                  pl.BlockSpec((eb, D_MODEL), lambda i, e: (e, 0)),
                  pl.BlockSpec((None, D_MODEL, eb), lambda i, e: (e, 0, 0)),
                  rt_blk, rt_blk, key_blk, key_blk,
                  pl.BlockSpec((tb, D_MODEL), lambda i, e: (i, 0)),
                  pl.BlockSpec((None, 6, D_MODEL), lambda i, e: (i // per_batch, 0, 0)),
                  pl.BlockSpec((1, D_MODEL), lambda i, e: (0, 0))],
        out_specs=pl.BlockSpec((tb, D_MODEL), lambda i, e: (i, 0)),
        scratch_shapes=[pltpu.VMEM((eb, tb), F32),
                        pltpu.VMEM((eb, tb), BF16),
                        pltpu.VMEM((D_MODEL, tb), F32)],
        compiler_params=_params(("arbitrary", "arbitrary")),
        name="peer",
    )(h2, u, vt, r1t, bt, nt, at, x1, ada3, final_gain)


def kernel(x, c, w_ada, b_ada, norm1, w_in, pool_w, pool_scale, lb_logits, hg_norm, w_out,
           norm2, peer_wq, peer_keys, peer_u, peer_v, final_norm):
    batch, seq, d = x.shape
    assert d == D_MODEL and w_ada.shape[0] == 1 and lb_logits.shape[0] == 2
    assert seq % HG_CHUNK == 0 and seq >= POOL_HISTORY
    t = _tiles(batch, seq)
    x2d = x.reshape(batch * seq, d)
    row = lambda v: v.reshape(1, -1).astype(F32)

    ada3 = _ada(c, w_ada[0], b_ada[0].reshape(1, -1), t["ada_tn"]).reshape(batch, 6, d)
    proj = _proj(x2d, ada3, row(norm1[0]), w_in[0].astype(BF16), seq, t["proj_tm"], t["proj_tn"])
    merged = _mixer(proj, lb_logits, row(hg_norm[0]), pool_w[0].astype(BF16), row(pool_scale[0]),
                    jnp.asarray(_decay_sum_matrix(), BF16), batch, seq, t["mix_ts"])
    keys = peer_keys[0].reshape(2 * PEER_HEADS, PEER_NKEYS, -1).astype(BF16)
    x1, h2, scores_t = _out(merged, x2d, ada3, w_out[0].astype(BF16), row(norm2[0]),
                            peer_wq[0].astype(BF16), keys, seq, t["out_tm"])
    r1t, bt, nt, at = _route(scores_t, t["route_tl"])
    eb = t["peer_eb"]
    u_scaled = (peer_u[0] * INV_SQRT2).astype(BF16)
    vt = peer_v[0].reshape(PEER_EXPERTS // eb, eb, d).transpose(0, 2, 1).astype(BF16)
    y = _peer(h2, u_scaled, vt, r1t, bt, nt, at, x1, ada3, row(final_norm), seq, t["peer_tb"], eb)
    return y.reshape(batch, seq, d)
```

```python
import functools

import numpy as np
import jax
import jax.numpy as jnp
from jax import lax
from jax.experimental import pallas as pl
from jax.experimental.pallas import tpu as pltpu

F32 = jnp.float32
BF16 = jnp.bfloat16
EPS = 1e-6

D_MODEL = 2048
LANES = 128
SUBLANES = 8
BF16_ROWS = 2 * SUBLANES
POOL_WINDOWS = (2, 4, 8, 16)
POOL_WIDTH = D_MODEL // 2
POOL_GROUP_DIM = POOL_WIDTH // len(POOL_WINDOWS)
POOL_OUT_DIM = D_MODEL // len(POOL_WINDOWS)
POOL_HISTORY = 16
HG_HEADS = 16
HG_DIM = 128
HG_CHUNK = 64
HG_LEVELS = (32, 16, 8, 4, 2, 1)
PEER_HEADS = 8
PEER_NKEYS = 128
PEER_TOPK = 16
PEER_EXPERTS = PEER_NKEYS * PEER_NKEYS
INV_SQRT2 = 0.7071067811865476

OFF_POOL = 0
OFF_Q = OFF_POOL + POOL_WIDTH
OFF_F = OFF_Q + D_MODEL
OFF_I = OFF_F + D_MODEL
OFF_G = OFF_I + D_MODEL
OFF_GA = OFF_G + D_MODEL
OFF_GB = OFF_GA + D_MODEL
IN_WIDTH = OFF_GB + D_MODEL

VMEM_LIMIT_BYTES = 58 * 1024 * 1024

NT_DIMS = (((1,), (1,)), ((), ()))
TN_DIMS = (((0,), (0,)), ((), ()))


def _tiles(batch, seq):
    del batch
    return dict(
        ada_tk=256,
        proj_tm=min(1024, seq), proj_tn=1024,
        mix_ts=min(256, seq),
        out_tm=min(512, seq),
        route_tl=LANES,
        peer_tb=min(512, seq), peer_eb=1024,
    )


def _params(semantics):
    return pltpu.CompilerParams(dimension_semantics=semantics, vmem_limit_bytes=VMEM_LIMIT_BYTES)


def _rms(x):
    return x * lax.rsqrt(jnp.mean(x * x, axis=-1, keepdims=True) + EPS)


def _ada_kernel(c_ref, w_ref, b_ref, o_ref):
    @pl.when(pl.program_id(0) == 0)
    def _():
        o_ref[...] = jnp.broadcast_to(b_ref[...], o_ref.shape)

    c = c_ref[...]
    cs = c * jax.nn.sigmoid(c)
    o_ref[...] += jnp.dot(cs.astype(BF16), w_ref[...].astype(BF16), preferred_element_type=F32)


def _ada(c, w, b, tk):
    bsz, n = c.shape[0], w.shape[1]
    return pl.pallas_call(
        _ada_kernel,
        out_shape=jax.ShapeDtypeStruct((bsz, n), F32),
        grid=(D_MODEL // tk,),
        in_specs=[pl.BlockSpec((bsz, tk), lambda k: (0, k)),
                  pl.BlockSpec((tk, n), lambda k: (k, 0)),
                  pl.BlockSpec((1, n), lambda k: (0, 0))],
        out_specs=pl.BlockSpec((bsz, n), lambda k: (0, 0)),
        compiler_params=_params(("arbitrary",)),
        name="ada",
    )(c, w, b)


def _proj_kernel(x_ref, ada_ref, g_ref, w_ref, o_ref, h_ref):
    @pl.when(pl.program_id(1) == 0)
    def _():
        y = _rms(x_ref[...]) * g_ref[...]
        h_ref[...] = (y * (1.0 + ada_ref[1:2, :]) + ada_ref[0:1, :]).astype(BF16)

    o_ref[...] = jnp.dot(h_ref[...], w_ref[...], preferred_element_type=F32).astype(o_ref.dtype)


def _proj(x2d, ada3, gain, w, seq, tm, tn):
    tokens, n = x2d.shape[0], w.shape[1]
    per_batch = seq // tm
    return pl.pallas_call(
        _proj_kernel,
        out_shape=jax.ShapeDtypeStruct((tokens, n), BF16),
        grid=(tokens // tm, n // tn),
        in_specs=[pl.BlockSpec((tm, D_MODEL), lambda i, j: (i, 0)),
                  pl.BlockSpec((None, 6, D_MODEL), lambda i, j: (i // per_batch, 0, 0)),
                  pl.BlockSpec((1, D_MODEL), lambda i, j: (0, 0)),
                  pl.BlockSpec((D_MODEL, tn), lambda i, j: (0, j))],
        out_specs=pl.BlockSpec((tm, tn), lambda i, j: (i, j)),
        scratch_shapes=[pltpu.VMEM((tm, D_MODEL), BF16)],
        compiler_params=_params(("arbitrary", "arbitrary")),
        name="proj",
    )(x2d, ada3, gain, w)


def _decay_sum_matrix():
    c = HG_CHUNK
    groups = []
    for m in HG_LEVELS:
        g = np.zeros((c, c), np.float32)
        for t in range(c):
            ref = (t // (2 * m)) * 2 * m + m - 1
            if (t // m) % 2 == 1:
                g[t, ref + 1:t + 1] = 1.0
            else:
                g[t, t + 1:ref + 1] = 1.0
        groups.append(g)
    groups.append(np.tril(np.ones((c, c), np.float32)))
    groups.append(np.triu(np.ones((c, c), np.float32), 1))
    n = np.concatenate(groups, axis=0)
    return np.concatenate([n, n, n], axis=1)


def _forget_gate(z, lb):
    t = jnp.exp(-jnp.abs(z))
    r = 1.0 / (1.0 + t)
    pos = z >= 0.0
    sig_pos = jnp.where(pos, r, t * r)
    sig_neg = jnp.where(pos, t * r, r)
    return jnp.log(lb + (1.0 - lb) * sig_pos), (1.0 - lb) * sig_neg


def _mixer_kernel(proj_ref, lbl_ref, hgn_ref, pw_ref, ps_ref, n_ref, o_ref,
                  st_ref, pbuf_ref, ya_ref, lf3_ref, kk_ref, e_ref, attn_ref, oi_ref, *, ts):
    sblk = pl.program_id(1)
    hist = POOL_HISTORY

    @pl.when(sblk == 0)
    def _():
        st_ref[...] = jnp.zeros_like(st_ref)
        pbuf_ref[0:hist, :] = jnp.zeros((hist, POOL_WIDTH), F32)

    pbuf_ref[hist:hist + ts, :] = proj_ref[:, OFF_POOL:OFF_POOL + POOL_WIDTH].astype(F32)
    pos1 = (sblk * ts + lax.broadcasted_iota(jnp.int32, (ts, 1), 0) + 1).astype(F32)
    for g, w in enumerate(POOL_WINDOWS):
        cols = slice(g * POOL_GROUP_DIM, (g + 1) * POOL_GROUP_DIM)
        cur = pbuf_ref[hist:hist + ts, cols]
        acc = cur
        for back in range(1, w):
            acc = acc + pbuf_ref[hist - back:hist - back + ts, cols]
        pooled = acc / jnp.minimum(pos1, float(w)) - cur
        oc = slice(g * POOL_OUT_DIM, (g + 1) * POOL_OUT_DIM)
        ya_ref[:, oc] = jnp.dot(pooled.astype(BF16), pw_ref[g], preferred_element_type=F32) * ps_ref[:, oc]
    pbuf_ref[0:hist, :] = pbuf_ref[ts:ts + hist, :]

    lbl = lbl_ref[...]
    lbe = jnp.exp(lbl - jnp.max(lbl, axis=0, keepdims=True))
    lb_all = lbe[0:1, :] / jnp.sum(lbe, axis=0, keepdims=True)

    c = HG_CHUNK
    nmat = n_ref[...]
    row = lax.broadcasted_iota(jnp.int32, (c, HG_DIM), 0)
    row_c = lax.broadcasted_iota(jnp.int32, (c, c), 0)
    col_c = lax.broadcasted_iota(jnp.int32, (c, c), 1)

    def chunk_body(ci, carry):
        rows = pl.ds(pl.multiple_of(ci * c, c), c)

        def seg(off, h):
            return proj_ref[rows, off + h * HG_DIM:off + (h + 1) * HG_DIM].astype(F32)

        for hp in range(0, HG_HEADS, 2):
            for h in (hp, hp + 1):
                hc = slice(h * HG_DIM, (h + 1) * HG_DIM)
                lf, kk = _forget_gate(seg(OFF_F, h), lb_all[:, hc])
                hi = lf.astype(BF16)
                rem = lf - hi.astype(F32)
                mid = rem.astype(BF16)
                lf3_ref[0:c, hc] = hi
                lf3_ref[c:2 * c, hc] = mid
                lf3_ref[2 * c:3 * c, hc] = (rem - mid.astype(F32)).astype(BF16)
                kk_ref[:, hc] = kk
            pc = slice(hp * HG_DIM, (hp + 2) * HG_DIM)
            e_ref[:, pc] = jnp.dot(nmat, lf3_ref[:, pc], preferred_element_type=F32)

        for h in range(HG_HEADS):
            hc = slice(h * HG_DIM, (h + 1) * HG_DIM)
            q, kk, iv = seg(OFF_Q, h), kk_ref[:, hc], seg(OFF_I, h)
            attn = jnp.zeros((c, c), F32)
            for li, m in enumerate(HG_LEVELS):
                right = (row & m) != 0
                xs = jnp.where(right, q, kk) * jnp.exp(e_ref[li * c:(li + 1) * c, hc])
                ql = jnp.where(right, xs, 0.0).astype(BF16)
                kl = jnp.where(right, 0.0, xs).astype(BF16)
                a = lax.dot_general(ql, kl, NT_DIMS, preferred_element_type=F32)
                if 2 * m < c:
                    sh = int(np.log2(2 * m))
                    a = jnp.where((row_c >> sh) == (col_c >> sh), a, 0.0)
                attn = attn + a
            attn_ref[h] = attn.astype(BF16)
            dc = jnp.exp(e_ref[6 * c:7 * c, hc])
            dr = jnp.exp(e_ref[7 * c:8 * c, hc])
            st = st_ref[h]
            carried = lax.dot_general((q * dc).astype(BF16), st.astype(BF16), NT_DIMS,
                                      preferred_element_type=F32)
            oi_ref[:, hc] = carried + jnp.sum(q * kk, axis=-1, keepdims=True) * iv
            upd = lax.dot_general(iv.astype(BF16), (kk * dr).astype(BF16), TN_DIMS,
                                  preferred_element_type=F32)
            st_ref[h] = st * dc[c - 1:c, :] + upd

        for h in range(HG_HEADS):
            hc = slice(h * HG_DIM, (h + 1) * HG_DIM)
            o = jnp.dot(attn_ref[h], seg(OFF_I, h).astype(BF16), preferred_element_type=F32) + oi_ref[:, hc]
            g = seg(OFF_G, h)
            yb = _rms(o) * hgn_ref[:, hc] * (g * jax.nn.sigmoid(g))
            merged = (jax.nn.sigmoid(seg(OFF_GA, h)) * ya_ref[rows, hc]
                      + jax.nn.sigmoid(seg(OFF_GB, h)) * yb)
            o_ref[rows, hc] = merged.astype(o_ref.dtype)
        return carry

    lax.fori_loop(0, ts // c, chunk_body, 0)


def _mixer(proj, lb_logits, hg_norm, pool_w, pool_scale, nmat, batch, seq, ts):
    tokens = proj.shape[0]
    per_batch = seq // ts
    return pl.pallas_call(
        functools.partial(_mixer_kernel, ts=ts),
        out_shape=jax.ShapeDtypeStruct((tokens, D_MODEL), BF16),
        grid=(batch, per_batch),
        in_specs=[pl.BlockSpec((ts, IN_WIDTH), lambda b, s: (b * per_batch + s, 0)),
                  pl.BlockSpec(lb_logits.shape, lambda b, s: (0, 0)),
                  pl.BlockSpec((1, D_MODEL), lambda b, s: (0, 0)),
                  pl.BlockSpec(pool_w.shape, lambda b, s: (0, 0, 0)),
                  pl.BlockSpec((1, D_MODEL), lambda b, s: (0, 0)),
                  pl.BlockSpec(nmat.shape, lambda b, s: (0, 0))],
        out_specs=pl.BlockSpec((ts, D_MODEL), lambda b, s: (b * per_batch + s, 0)),
        scratch_shapes=[pltpu.VMEM((HG_HEADS, HG_DIM, HG_DIM), F32),
                        pltpu.VMEM((POOL_HISTORY + ts, POOL_WIDTH), F32),
                        pltpu.VMEM((ts, D_MODEL), F32),
                        pltpu.VMEM((3 * HG_CHUNK, D_MODEL), BF16),
                        pltpu.VMEM((HG_CHUNK, D_MODEL), F32),
                        pltpu.VMEM((nmat.shape[0], D_MODEL), F32),
                        pltpu.VMEM((HG_HEADS, HG_CHUNK, HG_CHUNK), BF16),
                        pltpu.VMEM((HG_CHUNK, D_MODEL), F32)],
        compiler_params=_params(("arbitrary", "arbitrary")),
        name="mixer",
    )(proj, lb_logits, hg_norm, pool_w, pool_scale, nmat)


def _out_kernel(m_ref, x_ref, ada_ref, wo_ref, g_ref, wq_ref, keys_ref, x1_ref, h2_ref, sc_ref):
    y = jnp.dot(m_ref[...], wo_ref[...], preferred_element_type=F32)
    x1 = x_ref[...] + (1.0 + ada_ref[2:3, :]) * y
    x1_ref[...] = x1
    h2 = ((_rms(x1) * g_ref[...]) * (1.0 + ada_ref[4:5, :]) + ada_ref[3:4, :]).astype(BF16)
    h2_ref[...] = h2
    q = jnp.dot(h2, wq_ref[...], preferred_element_type=F32)
    for hp in range(2 * PEER_HEADS):
        rows = slice(hp * PEER_NKEYS, (hp + 1) * PEER_NKEYS)
        qs = q[:, hp * LANES:(hp + 1) * LANES].astype(BF16)
        sc_ref[rows, :] = lax.dot_general(keys_ref[hp], qs, NT_DIMS, preferred_element_type=F32)


def _out(merged, x2d, ada3, w_out, gain, w_q, keys, seq, tm):
    tokens = x2d.shape[0]
    per_batch = seq // tm
    nsc = 2 * PEER_HEADS * PEER_NKEYS
    return pl.pallas_call(
        _out_kernel,
        out_shape=(jax.ShapeDtypeStruct((tokens, D_MODEL), F32),
                   jax.ShapeDtypeStruct((tokens, D_MODEL), BF16),
                   jax.ShapeDtypeStruct((nsc, tokens), F32)),
        grid=(tokens // tm,),
        in_specs=[pl.BlockSpec((tm, D_MODEL), lambda i: (i, 0)),
                  pl.BlockSpec((tm, D_MODEL), lambda i: (i, 0)),
                  pl.BlockSpec((None, 6, D_MODEL), lambda i: (i // per_batch, 0, 0)),
                  pl.BlockSpec((D_MODEL, D_MODEL), lambda i: (0, 0), pipeline_mode=pl.Buffered(1)),
                  pl.BlockSpec((1, D_MODEL), lambda i: (0, 0)),
                  pl.BlockSpec(w_q.shape, lambda i: (0, 0), pipeline_mode=pl.Buffered(1)),
                  pl.BlockSpec(keys.shape, lambda i: (0, 0, 0))],
        out_specs=(pl.BlockSpec((tm, D_MODEL), lambda i: (i, 0)),
                   pl.BlockSpec((tm, D_MODEL), lambda i: (i, 0)),
                   pl.BlockSpec((nsc, tm), lambda i: (0, i))),
        compiler_params=_params(("arbitrary",)),
        name="out",
    )(merged, x2d, ada3, w_out, gain, w_q, keys)


def _sort_pairs(n):
    pairs = []
    p = 1
    while p < n:
        k = p
        while k >= 1:
            for j in range(k % p, n - k, 2 * k):
                for i in range(min(k, n - j - k)):
                    if (i + j) // (2 * p) == (i + j + k) // (2 * p):
                        pairs.append((i + j, i + j + k))
            k //= 2
        p *= 2
    return pairs


def _compare_exchange(vals, i, j):
    a, b = vals[i], vals[j]
    if b is None:
        return
    if a is None:
        vals[i], vals[j] = b, None
        return
    vals[i], vals[j] = jnp.maximum(a, b), jnp.minimum(a, b)


def _top16_sorted(blocks):
    k = PEER_TOPK
    vals = list(blocks) + [None] * (k - len(blocks))
    for i, j in _sort_pairs(k):
        _compare_exchange(vals, i, j)
    shift = SUBLANES // 2
    while shift >= 1:
        other = [None if v is None else pltpu.roll(v, shift, axis=0) for v in vals]
        merged = []
        for i in range(k):
            a, b = vals[i], other[k - 1 - i]
            merged.append(b if a is None else a if b is None else jnp.maximum(a, b))
        vals = merged
        d = k // 2
        while d >= 1:
            for i in range(k):
                if i & d == 0:
                    _compare_exchange(vals, i, i + d)
            d //= 2
        shift //= 2
    return vals


def _route_kernel(sc_ref, r1_ref, b_ref, n_ref, a_ref):
    tl = sc_ref.shape[1]
    k = PEER_TOPK
    nblk = PEER_NKEYS // SUBLANES
    sub = lax.broadcasted_iota(jnp.int32, (SUBLANES, tl), 0)

    def rows_from(vals):
        out = vals[0]
        for r in range(1, SUBLANES):
            out = jnp.where(sub == r, vals[r], out)
        return out

    for h in range(PEER_HEADS):
        base0, base1 = (2 * h) * PEER_NKEYS, (2 * h + 1) * PEER_NKEYS
        s0 = [sc_ref[base0 + q * SUBLANES:base0 + (q + 1) * SUBLANES, :] for q in range(nblk)]
        s1 = [sc_ref[base1 + q * SUBLANES:base1 + (q + 1) * SUBLANES, :] for q in range(nblk)]
        v0 = _top16_sorted(s0)
        v1 = _top16_sorted(s1)
        v1_lo, v1_hi, v0_hi = rows_from(v1[0:8]), rows_from(v1[8:16]), rows_from(v0[8:16])
        cands = [v0[0] + v1_lo, v0[0] + v1_hi]
        cands += [v0[a] + v1_lo for a in range(1, 8)]
        cands += [v0_hi + v1[0]]
        tau = _top16_sorted(cands)[k - 1]
        top = v0[0] + v1[0]
        sel = [cc >= tau for cc in cands]
        z = functools.reduce(jnp.add, [jnp.where(ss, jnp.exp(cc - top), 0.0) for ss, cc in zip(sel, cands)])
        z = jnp.sum(z, axis=0, keepdims=True)
        cnts = [jnp.where(ss, 1.0, 0.0) for ss in sel]
        n_rows = [jnp.sum(cnts[0] + cnts[1], axis=0, keepdims=True)]
        n_rows += [jnp.sum(cnts[a + 1], axis=0, keepdims=True) for a in range(1, 8)]
        n_rows += [cnts[9][a:a + 1, :] for a in range(8)]
        inv_z = INV_SQRT2 / z
        for q in range(0, nblk, 2):
            rank, count = [], []
            for qq in (q, q + 1):
                r = jnp.zeros((SUBLANES, tl), F32)
                for kk in range(k):
                    r = r + jnp.where(v1[kk] > s1[qq], 1.0, 0.0)
                rank.append(r)
                c = jnp.zeros((SUBLANES, tl), F32)
                for a in range(k):
                    c = jnp.where(s0[qq] == v0[a], n_rows[a], c)
                count.append(c)
            rows = slice(q * SUBLANES, (q + 2) * SUBLANES)
            packed = slice((h * PEER_NKEYS + q * SUBLANES) // 2, (h * PEER_NKEYS + (q + 2) * SUBLANES) // 2)
            r1_ref[packed, :] = pltpu.bitcast(jnp.concatenate(rank, axis=0).astype(BF16), jnp.uint32)
            gate1 = jnp.exp(jnp.concatenate([s1[q], s1[q + 1]], axis=0) - v1[0][0:1]) * inv_z
            b_ref[packed, :] = pltpu.bitcast(gate1.astype(BF16), jnp.uint32)
            n_ref[h, rows, :] = jnp.concatenate(count, axis=0)
            a_ref[h, rows, :] = jnp.exp(jnp.concatenate([s0[q], s0[q + 1]], axis=0) - v0[0][0:1])


def _route(scores_t, tl):
    tokens = scores_t.shape[1]
    shp = (PEER_HEADS, PEER_NKEYS, tokens)
    blk = pl.BlockSpec((PEER_HEADS, PEER_NKEYS, tl), lambda i: (0, 0, i))
    flat_shp = (PEER_HEADS * PEER_NKEYS // 2, tokens)
    flat_blk = pl.BlockSpec((PEER_HEADS * PEER_NKEYS // 2, tl), lambda i: (0, i))
    return pl.pallas_call(
        _route_kernel,
        out_shape=(jax.ShapeDtypeStruct(flat_shp, jnp.uint32), jax.ShapeDtypeStruct(flat_shp, jnp.uint32),
                   jax.ShapeDtypeStruct(shp, F32), jax.ShapeDtypeStruct(shp, F32)),
        grid=(tokens // tl,),
        in_specs=[pl.BlockSpec((scores_t.shape[0], tl), lambda i: (0, i))],
        out_specs=(flat_blk, flat_blk, blk, blk),
        compiler_params=_params(("arbitrary",)),
        name="route",
    )(scores_t)


def _peer_kernel(h2_ref, u_ref, vt_ref, r1_ref, b_ref, n_ref, a_ref, x1_ref, ada_ref, fg_ref, o_ref,
                 hid_ref, p_ref, acc_ref, *, tb, eb):
    e = pl.program_id(1)
    sub = BF16_ROWS

    @pl.when(e == 0)
    def _():
        acc_ref[...] = jnp.zeros_like(acc_ref)

    hid_ref[...] = lax.dot_general(u_ref[...], h2_ref[...], NT_DIMS, preferred_element_type=F32)

    for ii in range(eb // PEER_NKEYS):
        for lc in range(tb // LANES):
            lanes = slice(lc * LANES, (lc + 1) * LANES)
            n_h = [jnp.broadcast_to(n_ref[h, ii:ii + 1, lanes], (sub, LANES)).astype(BF16)
                   for h in range(PEER_HEADS)]
            a_h = [jnp.broadcast_to(a_ref[h, ii:ii + 1, lanes], (sub, LANES)).astype(BF16)
                   for h in range(PEER_HEADS)]
            for g in range(PEER_NKEYS // sub):
                rr = slice(ii * PEER_NKEYS + g * sub, ii * PEER_NKEYS + (g + 1) * sub)
                gate = None
                for h in range(PEER_HEADS):
                    jr = slice((h * PEER_NKEYS + g * sub) // 2, (h * PEER_NKEYS + (g + 1) * sub) // 2)
                    rank = pltpu.bitcast(r1_ref[jr, lanes], BF16)
                    factor = pltpu.bitcast(b_ref[jr, lanes], BF16)
                    term = jnp.where(rank < n_h[h], factor, jnp.zeros((), BF16)) * a_h[h]
                    gate = term if gate is None else gate + term
                w = hid_ref[rr, lanes].astype(BF16)
                p_ref[rr, lanes] = gate * (w * (1.0 + lax.erf(w)))
    acc_ref[...] += jnp.dot(vt_ref[...], p_ref[...], preferred_element_type=F32)

    @pl.when(e == pl.num_programs(1) - 1)
    def _():
        x2 = x1_ref[...] + (1.0 + ada_ref[5:6, :]) * acc_ref[...].T
        o_ref[...] = _rms(x2) * fg_ref[...]


def _peer(h2, u, vt, r1t, bt, nt, at, x1, ada3, final_gain, seq, tb, eb):
    tokens = h2.shape[0]
    per_batch = seq // tb
    keys_per_tile = eb // PEER_NKEYS
    rt_blk = pl.BlockSpec((PEER_HEADS * PEER_NKEYS // 2, tb), lambda i, e: (0, i))
    key_blk = pl.BlockSpec((PEER_HEADS, keys_per_tile, tb), lambda i, e: (0, e, i))
    return pl.pallas_call(
        functools.partial(_peer_kernel, tb=tb, eb=eb),
        out_shape=jax.ShapeDtypeStruct((tokens, D_MODEL), F32),
        grid=(tokens // tb, PEER_EXPERTS // eb),
        in_specs=[pl.BlockSpec((tb, D_MODEL), lambda i, e: (i, 0)),
                  pl.BlockSpec((eb, D_MODEL), lambda i, e: (e, 0)),
                  pl.BlockSpec((None, D_MODEL, eb), lambda i, e: (e, 0, 0)),
                  rt_blk, rt_blk, key_blk, key_blk,
                  pl.BlockSpec((tb, D_MODEL), lambda i, e: (i, 0)),
                  pl.BlockSpec((None, 6, D_MODEL), lambda i, e: (i // per_batch, 0, 0)),
                  pl.BlockSpec((1, D_MODEL), lambda i, e: (0, 0))],
        out_specs=pl.BlockSpec((tb, D_MODEL), lambda i, e: (i, 0)),
        scratch_shapes=[pltpu.VMEM((eb, tb), F32),
                        pltpu.VMEM((eb, tb), BF16),
                        pltpu.VMEM((D_MODEL, tb), F32)],
        compiler_params=_params(("arbitrary", "arbitrary")),
        name="peer",
    )(h2, u, vt, r1t, bt, nt, at, x1, ada3, final_gain)


def kernel(x, c, w_ada, b_ada, norm1, w_in, pool_w, pool_scale, lb_logits, hg_norm, w_out,
           norm2, peer_wq, peer_keys, peer_u, peer_v, final_norm):
    batch, seq, d = x.shape
    assert d == D_MODEL and w_ada.shape[0] == 1 and lb_logits.shape[0] == 2
    assert seq % HG_CHUNK == 0 and seq >= POOL_HISTORY
    t = _tiles(batch, seq)
    x2d = x.reshape(batch * seq, d)
    row = lambda v: v.reshape(1, -1).astype(F32)

    ada3 = _ada(c, w_ada[0], b_ada[0].reshape(1, -1), t["ada_tk"]).reshape(batch, 6, d)
    proj = _proj(x2d, ada3, row(norm1[0]), w_in[0].astype(BF16), seq, t["proj_tm"], t["proj_tn"])
    merged = _mixer(proj, lb_logits, row(hg_norm[0]), pool_w[0].astype(BF16), row(pool_scale[0]),
                    jnp.asarray(_decay_sum_matrix(), BF16), batch, seq, t["mix_ts"])
    keys = peer_keys[0].reshape(2 * PEER_HEADS, PEER_NKEYS, -1).astype(BF16)
    x1, h2, scores_t = _out(merged, x2d, ada3, w_out[0].astype(BF16), row(norm2[0]),
                            peer_wq[0].astype(BF16), keys, seq, t["out_tm"])
    r1t, bt, nt, at = _route(scores_t, t["route_tl"])
    eb = t["peer_eb"]
    u_scaled = (peer_u[0] * INV_SQRT2).astype(BF16)
    vt = peer_v[0].reshape(PEER_EXPERTS // eb, eb, d).transpose(0, 2, 1).astype(BF16)
    y = _peer(h2, u_scaled, vt, r1t, bt, nt, at, x1, ada3, row(final_norm), seq, t["peer_tb"], eb)
    return y.reshape(batch, seq, d)
```
